```python
import jax, jax.numpy as jnp
from jax import lax
import numpy as np

D_MODEL = 1024
BATCH = 16
SEQ = 4096
DEPTH = 1

MEM_LEN = 256
HEAD_DIM = 64
CONV_CH = D_MODEL // 4
ATTN_HEADS = (3 * D_MODEL // 4) // HEAD_DIM
ATTN_WIDTH = ATTN_HEADS * HEAD_DIM
MIX_WIDTH = CONV_CH + ATTN_WIDTH
IN_WIDTH = 2 * CONV_CH + 3 * ATTN_WIDTH
CONV_KERNEL = 31
DILATED_BRANCHES = ((128, 1), (512, 4), (2048, 16))
ROPE_THETA = 10000.0
MEM_HEADS = 4
MEM_HEAD_DIM = D_MODEL // MEM_HEADS
N_GROUPS = 4
EXPERTS_PER_GROUP = 8
N_EXPERTS = N_GROUPS * EXPERTS_PER_GROUP
TOP_K = 2
EXPERT_FF = D_MODEL // 2
DISPATCH_BLOCK = 256
NORM_EPS = 1e-6
LN_EPS = 1e-5

kernel_name = 'hybrid_conv_dilated_attn_hmoe_encoder'


def rms_norm(t, g):
    tf = t.astype(jnp.float32)
    y = tf * lax.rsqrt(jnp.mean(tf * tf, axis=-1, keepdims=True) + NORM_EPS)
    return (y * g.astype(jnp.float32)).astype(t.dtype)


def layer_norm(t, g, b):
    tf = t.astype(jnp.float32)
    mu = jnp.mean(tf, axis=-1, keepdims=True)
    var = jnp.mean(jnp.square(tf - mu), axis=-1, keepdims=True)
    y = (tf - mu) * lax.rsqrt(var + LN_EPS)
    return (y * g.astype(jnp.float32) + b.astype(jnp.float32)).astype(t.dtype)


def rotary(t, positions):
    half = t.shape[-1] // 2
    inv = 1.0 / (ROPE_THETA ** (jnp.arange(half, dtype=jnp.float32) * (2.0 / t.shape[-1])))
    ang = positions.astype(jnp.float32)[:, None] * inv[None, :]
    cos, sin = jnp.cos(ang), jnp.sin(ang)
    t1, t2 = t[..., :half], t[..., half:]
    return jnp.concatenate([t1 * cos - t2 * sin, t1 * sin + t2 * cos], axis=-1)


def conformer_conv(conv_in, dw_w, dw_b, ln_g, ln_b):
    a, gate = conv_in[..., :CONV_CH], conv_in[..., CONV_CH:]
    c = a * jax.nn.sigmoid(gate)
    pad = CONV_KERNEL // 2
    c = lax.conv_general_dilated(
        c, dw_w.reshape(CONV_KERNEL, 1, CONV_CH).astype(c.dtype),
        window_strides=(1,), padding=[(pad, pad)],
        dimension_numbers=('NWC', 'WIO', 'NWC'),
        feature_group_count=CONV_CH) + dw_b.astype(c.dtype)
    c = layer_norm(c, ln_g, ln_b)
    return jax.nn.silu(c)


def dilated_branch(q, k, v, window, dilation):
    B_, H, S_, Dh = q.shape
    half = window // (2 * dilation)
    blk = half
    L = S_ // dilation
    nb = -(-L // blk)
    Lp = nb * blk
    pad = Lp - L

    def to_sub(t):
        return t.reshape(B_, H, L, dilation, Dh).transpose(0, 1, 3, 2, 4)

    qs, ks, vs = to_sub(q), to_sub(k), to_sub(v)
    qs = jnp.pad(qs, ((0, 0), (0, 0), (0, 0), (0, pad), (0, 0)))
    kp = jnp.pad(ks, ((0, 0), (0, 0), (0, 0), (blk, pad + blk), (0, 0)))
    vp = jnp.pad(vs, ((0, 0), (0, 0), (0, 0), (blk, pad + blk), (0, 0)))
    qb = qs.reshape(B_, H, dilation, nb, blk, Dh)
    kb = kp.reshape(B_, H, dilation, nb + 2, blk, Dh)
    vb = vp.reshape(B_, H, dilation, nb + 2, blk, Dh)
    kw = jnp.concatenate([kb[:, :, :, :-2], kb[:, :, :, 1:-1], kb[:, :, :, 2:]], axis=-2)
    vw = jnp.concatenate([vb[:, :, :, :-2], vb[:, :, :, 1:-1], vb[:, :, :, 2:]], axis=-2)

    qi = np.arange(nb)[:, None] * blk + np.arange(blk)[None, :]
    ki = np.arange(nb)[:, None] * blk - blk + np.arange(3 * blk)[None, :]
    valid = ((np.abs(qi[:, :, None] - ki[:, None, :]) <= half)
             & (ki[:, None, :] >= 0) & (ki[:, None, :] < L))

    s = jnp.einsum('bhrnqd,bhrnkd->bhrnqk', qb, kw) * (HEAD_DIM ** -0.5)
    s = jnp.where(jnp.asarray(valid), s, -jnp.inf)
    m = jnp.max(s, axis=-1)
    p = jnp.exp(s - m[..., None])
    den = jnp.sum(p, axis=-1)
    num = jnp.einsum('bhrnqk,bhrnkd->bhrnqd', p, vw)

    def back(t):
        tail = tuple(t.shape[5:])
        t = t.reshape((B_, H, dilation, Lp) + tail)[:, :, :, :L]
        perm = (0, 1, 3, 2) + tuple(range(4, 4 + len(tail)))
        return t.transpose(perm).reshape((B_, H, S_) + tail)

    return back(num), back(m), back(den)


def dilated_attention(q, k, v, positions):
    B_, H, S_, Dh = q.shape
    out_dtype = q.dtype
    qf = rotary(q.astype(jnp.float32), positions)
    kf = rotary(k.astype(jnp.float32), positions)
    vf = v.astype(jnp.float32)
    num_acc = den_acc = m_acc = None
    for window, dilation in DILATED_BRANCHES:
        num, m, den = dilated_branch(qf, kf, vf, window, dilation)
        if m_acc is None:
            num_acc, den_acc, m_acc = num, den, m
        else:
            m_new = jnp.maximum(m_acc, m)
            a = jnp.exp(m_acc - m_new)
            b = jnp.exp(m - m_new)
            num_acc = num_acc * a[..., None] + num * b[..., None]
            den_acc = den_acc * a + den * b
            m_acc = m_new
    o = num_acc / den_acc[..., None]
    return o.transpose(0, 2, 1, 3).reshape(B_, S_, H * Dh).astype(out_dtype)


def memory_cross_attention(h, mem_n, w_q, w_k, w_v, w_o):
    B_, S_, D = h.shape
    q = (h @ w_q).reshape(B_, S_, MEM_HEADS, MEM_HEAD_DIM)
    k = (mem_n @ w_k).reshape(B_, -1, MEM_HEADS, MEM_HEAD_DIM)
    v = (mem_n @ w_v).reshape(B_, -1, MEM_HEADS, MEM_HEAD_DIM)
    s = jnp.einsum('bshd,bmhd->bhsm', q, k, preferred_element_type=jnp.float32) * (MEM_HEAD_DIM ** -0.5)
    p = jax.nn.softmax(s, axis=-1).astype(v.dtype)
    o = jnp.einsum('bhsm,bmhd->bshd', p, v).reshape(B_, S_, D)
    return o @ w_o


def hierarchical_moe(h, w_group, b_group, w_router, b_router, w1, w3, w2):
    B_, S_, D = h.shape
    T = B_ * S_
    hf = h.reshape(T, D)
    gl = (hf @ w_group + b_group).astype(jnp.float32)
    gp = jax.nn.softmax(gl, axis=-1)
    _, g_sel = lax.top_k(gl, 1)
    g_idx = g_sel[:, 0]
    tok_idx = jnp.arange(T, dtype=jnp.int32)
    p_g = gp[tok_idx, g_idx]
    el = (hf @ w_router + b_router).astype(jnp.float32).reshape(T, N_GROUPS, EXPERTS_PER_GROUP)
    el_sel = el[tok_idx, g_idx]
    top_v, top_i = lax.top_k(el_sel, TOP_K)
    gates = p_g[:, None] * jax.nn.softmax(top_v, axis=-1)
    expert_ids = g_idx[:, None] * EXPERTS_PER_GROUP + top_i

    A = T * TOP_K
    flat_e = expert_ids.reshape(A).astype(jnp.int32)
    flat_t = jnp.repeat(tok_idx, TOP_K)
    flat_g = gates.reshape(A)
    order = jnp.argsort(flat_e)
    se, st, sg = flat_e[order], flat_t[order], flat_g[order]
    counts = jnp.bincount(flat_e, length=N_EXPERTS).astype(jnp.int32)
    starts = jnp.cumsum(counts) - counts
    pcounts = (counts + DISPATCH_BLOCK - 1) // DISPATCH_BLOCK * DISPATCH_BLOCK
    pstarts = jnp.cumsum(pcounts) - pcounts
    dest = pstarts[se] + jnp.arange(A, dtype=jnp.int32) - starts[se]
    P = (-(-A // DISPATCH_BLOCK) + N_EXPERTS) * DISPATCH_BLOCK
    n_blk = P // DISPATCH_BLOCK
    row_tok = jnp.full((P,), T, dtype=jnp.int32).at[dest].set(st)
    row_gate = jnp.zeros((P,), jnp.float32).at[dest].set(sg)
    blk_e = jnp.minimum(
        jnp.searchsorted(pstarts + pcounts, jnp.arange(n_blk, dtype=jnp.int32) * DISPATCH_BLOCK,
                         side='right'),
        N_EXPERTS - 1).astype(jnp.int32)
    x_rows = jnp.concatenate([hf, jnp.zeros((1, D), hf.dtype)], axis=0)[row_tok]
    x_blocks = x_rows.reshape(n_blk, DISPATCH_BLOCK, D)

    def expert_block(args):
        xb, e = args
        return (jax.nn.silu(xb @ w1[e]) * (xb @ w3[e])) @ w2[e]

    y_rows = lax.map(expert_block, (x_blocks, blk_e)).reshape(P, D)
    y = jax.ops.segment_sum(y_rows * row_gate[:, None].astype(y_rows.dtype), row_tok,
                            num_segments=T + 1)[:T]
    return y.reshape(B_, S_, D)


def setup_inputs(seed: int = 0) -> dict:
    key = jax.random.key(seed)
    ks = jax.random.split(key, 26)
    f32 = jnp.float32
    D = D_MODEL

    def nrm(k, shape, scale):
        return jax.random.normal(k, shape, f32) * scale

    def gain(k, shape):
        return 1.0 + 0.02 * jax.random.normal(k, shape, f32)

    return {
        'x': jax.random.normal(ks[0], (BATCH, SEQ, D), f32),
        'mem': jax.random.normal(ks[1], (BATCH, MEM_LEN, D), f32),
        'positions': jnp.arange(SEQ, dtype=jnp.int32),
        'mix_norm_g': gain(ks[2], (DEPTH, D)),
        'w_in': nrm(ks[3], (DEPTH, D, IN_WIDTH), D ** -0.5),
        'conv_dw_w': nrm(ks[4], (DEPTH, CONV_KERNEL, CONV_CH), CONV_KERNEL ** -0.5),
        'conv_dw_b': nrm(ks[5], (DEPTH, CONV_CH), 0.02),
        'conv_ln_g': gain(ks[6], (DEPTH, CONV_CH)),
        'conv_ln_b': nrm(ks[7], (DEPTH, CONV_CH), 0.02),
        'conv_out_g': gain(ks[8], (DEPTH, CONV_CH)),
        'attn_out_g': gain(ks[9], (DEPTH, ATTN_WIDTH)),
        'w_out': nrm(ks[10], (DEPTH, MIX_WIDTH, D), MIX_WIDTH ** -0.5),
        'xattn_norm_g': gain(ks[11], (DEPTH, D)),
        'mem_norm_g': gain(ks[12], (DEPTH, D)),
        'w_xq': nrm(ks[13], (DEPTH, D, D), D ** -0.5),
        'w_xk': nrm(ks[14], (DEPTH, D, D), D ** -0.5),
        'w_xv': nrm(ks[15], (DEPTH, D, D), D ** -0.5),
        'w_xo': nrm(ks[16], (DEPTH, D, D), D ** -0.5),
        'moe_norm_g': gain(ks[17], (DEPTH, D)),
        'w_group': nrm(ks[18], (DEPTH, D, N_GROUPS), D ** -0.5),
        'b_group': nrm(ks[19], (DEPTH, N_GROUPS), 0.01),
        'w_router': nrm(ks[20], (DEPTH, D, N_EXPERTS), D ** -0.5),
        'b_router': nrm(ks[21], (DEPTH, N_EXPERTS), 0.01),
        'w1': nrm(ks[22], (DEPTH, N_EXPERTS, D, EXPERT_FF), D ** -0.5),
        'w3': nrm(ks[23], (DEPTH, N_EXPERTS, D, EXPERT_FF), D ** -0.5),
        'w2': nrm(ks[24], (DEPTH, N_EXPERTS, EXPERT_FF, D), EXPERT_FF ** -0.5),
        'final_norm_g': gain(ks[25], (D,)),
    }


def reference(x, mem, positions, mix_norm_g, w_in, conv_dw_w, conv_dw_b, conv_ln_g, conv_ln_b,
              conv_out_g, attn_out_g, w_out, xattn_norm_g, mem_norm_g, w_xq, w_xk, w_xv, w_xo,
              moe_norm_g, w_group, b_group, w_router, b_router, w1, w3, w2, final_norm_g):
    B_, S_, _ = x.shape
    for l in range(DEPTH):
        h = rms_norm(x, mix_norm_g[l])
        u = h @ w_in[l]
        conv_in = u[..., :2 * CONV_CH]
        qkv = u[..., 2 * CONV_CH:].reshape(B_, S_, 3, ATTN_HEADS, HEAD_DIM).transpose(2, 0, 3, 1, 4)
        c = conformer_conv(conv_in, conv_dw_w[l], conv_dw_b[l], conv_ln_g[l], conv_ln_b[l])
        a = dilated_attention(qkv[0], qkv[1], qkv[2], positions)
        mixed = jnp.concatenate([rms_norm(c, conv_out_g[l]), rms_norm(a, attn_out_g[l])], axis=-1)
        x = x + mixed @ w_out[l]
        h = rms_norm(x, xattn_norm_g[l])
        m = rms_norm(mem, mem_norm_g[l])
        x = x + memory_cross_attention(h, m, w_xq[l], w_xk[l], w_xv[l], w_xo[l])
        h = rms_norm(x, moe_norm_g[l])
        x = x + hierarchical_moe(h, w_group[l], b_group[l], w_router[l], b_router[l], w1[l], w3[l], w2[l])
    return rms_norm(x, final_norm_g)
```

```python
import functools

import jax
import jax.numpy as jnp
from jax import lax
from jax.experimental import pallas as pl
from jax.experimental.pallas import tpu as pltpu

HEAD_DIM = 64
CONV_CH = 256
ATTN_HEADS = 12
ATTN_WIDTH = ATTN_HEADS * HEAD_DIM
CONV_KERNEL = 31
CONV_PAD = 16
BAND = 64
DILATIONS = (1, 4, 16)
ROPE_THETA = 10000.0
MEM_HEADS = 4
N_GROUPS = 4
EXPERTS_PER_GROUP = 8
N_EXPERTS = N_GROUPS * EXPERTS_PER_GROUP
NORM_EPS = 1e-6
LN_EPS = 1e-5
LANES = 128
NEG = -1e30

TM_IN = 512
TM_POST = 256
TM_ROW = 256
BM_EXPERT = 256
CONV_CHUNK = 128
VMEM_LIMIT = 56 * 1024 * 1024

F32 = jnp.float32
BF16 = jnp.bfloat16


def _rms(x, g):
    return x * lax.rsqrt(jnp.mean(x * x, axis=-1, keepdims=True) + NORM_EPS) * g


def _dot(a, b):
    return jnp.dot(a, b, preferred_element_type=F32)


def _dot_nt(a, b):
    return lax.dot_general(a, b, (((1,), (1,)), ((), ())), preferred_element_type=F32)


def _rope_kernel(pos_ref, inv_ref, sign_ref, cos_ref, sin_ref):
    ang = pos_ref[...].astype(F32) * inv_ref[...]
    cos_ref[...] = jnp.cos(ang)
    sin_ref[...] = jnp.sin(ang) * sign_ref[...]


def _rope_tables(positions):
    s = positions.shape[0]
    half = HEAD_DIM // 2
    inv = 1.0 / (ROPE_THETA ** (jnp.arange(half, dtype=F32) * (2.0 / HEAD_DIM)))
    inv = jnp.tile(inv, LANES // half)[None, :]
    sign = jnp.tile(jnp.concatenate([-jnp.ones((half,), F32), jnp.ones((half,), F32)]),
                    LANES // HEAD_DIM)[None, :]
    return pl.pallas_call(
        _rope_kernel,
        out_shape=(jax.ShapeDtypeStruct((s, LANES), F32), jax.ShapeDtypeStruct((s, LANES), F32)),
        name="rope_tables",
    )(positions.reshape(s, 1), inv, sign)


def _inproj_kernel(x_ref, g_ref, w_ref, cos_ref, sin_ref, c_ref, q_ref, k_ref, v_ref):
    h = _rms(x_ref[...], g_ref[...]).astype(BF16)
    u = _dot(h, w_ref[:, 0:2 * CONV_CH])
    c_ref[...] = u[:, :CONV_CH] / (1.0 + jnp.exp(-u[:, CONV_CH:]))
    cos = cos_ref[...]
    sin = sin_ref[...]
    lane = lax.broadcasted_iota(jnp.int32, cos.shape, 1)
    first_half = (lane % HEAD_DIM) < (HEAD_DIM // 2)
    off = 2 * CONV_CH
    for ref, scale in ((q_ref, HEAD_DIM ** -0.5), (k_ref, 1.0)):
        u = _dot(h, w_ref[:, off:off + ATTN_WIDTH])
        off += ATTN_WIDTH
        for j in range(ATTN_WIDTH // LANES):
            xs = u[:, j * LANES:(j + 1) * LANES]
            partner = jnp.where(first_half, pltpu.roll(xs, LANES - 32, 1), pltpu.roll(xs, 32, 1))
            r = xs * cos + partner * sin
            ref[:, j * LANES:(j + 1) * LANES] = (r * scale).astype(BF16)
    v_ref[...] = _dot(h, w_ref[:, off:off + ATTN_WIDTH]).astype(BF16)


def _in_projection(xf, g, w_in, cos_t, sin_t, seq):
    t, d = xf.shape
    tm = TM_IN
    n_pos_blk = seq // tm
    row = lambda i: (i, 0)
    const = lambda i: (0, 0)
    return pl.pallas_call(
        _inproj_kernel,
        grid=(t // tm,),
        in_specs=[
            pl.BlockSpec((tm, d), row),
            pl.BlockSpec((1, d), const),
            pl.BlockSpec(w_in.shape, const),
            pl.BlockSpec((tm, LANES), lambda i: (i % n_pos_blk, 0)),
            pl.BlockSpec((tm, LANES), lambda i: (i % n_pos_blk, 0)),
        ],
        out_specs=[
            pl.BlockSpec((tm, CONV_CH), row),
            pl.BlockSpec((tm, ATTN_WIDTH), row),
            pl.BlockSpec((tm, ATTN_WIDTH), row),
            pl.BlockSpec((tm, ATTN_WIDTH), row),
        ],
        out_shape=[
            jax.ShapeDtypeStruct((t, CONV_CH), F32),
            jax.ShapeDtypeStruct((t, ATTN_WIDTH), BF16),
            jax.ShapeDtypeStruct((t, ATTN_WIDTH), BF16),
            jax.ShapeDtypeStruct((t, ATTN_WIDTH), BF16),
        ],
        compiler_params=pltpu.CompilerParams(
            dimension_semantics=("parallel",), vmem_limit_bytes=VMEM_LIMIT),
        name="in_projection",
    )(xf, g, w_in, cos_t, sin_t)


def _conv_kernel(c_ref, w_ref, b_ref, lng_ref, lnb_ref, og_ref, o_ref, pad_ref):
    seq = c_ref.shape[1]
    zeros = jnp.zeros((CONV_PAD, CONV_CH), F32)
    pad_ref[0:CONV_PAD, :] = zeros
    pad_ref[seq + CONV_PAD:seq + 2 * CONV_PAD, :] = zeros
    pad_ref[CONV_PAD:seq + CONV_PAD, :] = c_ref[0]
    shift = CONV_PAD - CONV_KERNEL // 2

    def body(i, carry):
        base = pl.multiple_of(i * CONV_CHUNK, CONV_CHUNK)
        win = pad_ref[pl.ds(base, CONV_CHUNK + 2 * CONV_PAD), :]
        acc = jnp.zeros((CONV_CHUNK, CONV_CH), F32)
        for b in range(8):
            taps = [k for k in range(CONV_KERNEL) if k % 8 == b]
            span = CONV_CHUNK + 8 * (len(taps) - 1)
            sb = win[b + shift:b + shift + span, :]
            for a, k in enumerate(taps):
                acc = acc + sb[8 * a:8 * a + CONV_CHUNK, :] * w_ref[k:k + 1, :]
        acc = acc + b_ref[...]
        mu = jnp.mean(acc, axis=-1, keepdims=True)
        cen = acc - mu
        var = jnp.mean(cen * cen, axis=-1, keepdims=True)
        y = cen * lax.rsqrt(var + LN_EPS) * lng_ref[...] + lnb_ref[...]
        y = y / (1.0 + jnp.exp(-y))
        o_ref[0, pl.ds(base, CONV_CHUNK), :] = _rms(y, og_ref[...]).astype(BF16)
        return carry

    lax.fori_loop(0, seq // CONV_CHUNK, body, 0)


def _conformer_conv(c, w, b, lng, lnb, og):
    bsz, seq, ch = c.shape
    vec = pl.BlockSpec((1, ch), lambda i: (0, 0))
    return pl.pallas_call(
        _conv_kernel,
        grid=(bsz,),
        in_specs=[pl.BlockSpec((1, seq, ch), lambda i: (i, 0, 0)),
                  pl.BlockSpec(w.shape, lambda i: (0, 0)), vec, vec, vec, vec],
        out_specs=pl.BlockSpec((1, seq, ch), lambda i: (i, 0, 0)),
        out_shape=jax.ShapeDtypeStruct((bsz, seq, ch), BF16),
        scratch_shapes=[pltpu.VMEM((seq + 2 * CONV_PAD, ch), F32)],
        compiler_params=pltpu.CompilerParams(
            dimension_semantics=("parallel",), vmem_limit_bytes=VMEM_LIMIT),
        name="conformer_conv",
    )(c, w, b, lng, lnb, og)


def _band_attn_kernel(q_ref, k_ref, v_ref, o_ref, lse_ref, *, sub_len, q_blk, win):
    q0 = pl.program_id(2) * q_blk
    ws = pl.multiple_of(jnp.clip(q0 - BAND, 0, sub_len - win), BAND)
    row = lax.broadcasted_iota(jnp.int32, (q_blk, win), 0)
    col = lax.broadcasted_iota(jnp.int32, (q_blk, win), 1)
    bias = jnp.where(jnp.abs(row - col + (q0 - ws)) <= BAND, 0.0, NEG).astype(F32)
    lane = lax.broadcasted_iota(jnp.int32, (q_blk, LANES), 1)
    is_a = lane < HEAD_DIM
    lse_tile = jnp.zeros((q_blk, LANES), F32)
    for hp in range(ATTN_WIDTH // LANES):
        sl = slice(hp * LANES, (hp + 1) * LANES)
        qh = q_ref[0, :, sl]
        kh = k_ref[0, pl.ds(ws, win), sl]
        vh = v_ref[0, pl.ds(ws, win), sl]
        zero = jnp.zeros_like(qh)
        outs = []
        for head, qx in enumerate((jnp.where(is_a, qh, zero), jnp.where(is_a, zero, qh))):
            s = _dot_nt(qx, kh) + bias
            m = jnp.max(s, axis=1, keepdims=True)
            p = jnp.exp(s - m)
            l = jnp.sum(p, axis=1, keepdims=True)
            outs.append(_dot(p.astype(BF16), vh) / l)
            lse_tile = jnp.where(lane == 2 * hp + head, m + jnp.log(l), lse_tile)
        o_ref[0, :, sl] = jnp.where(is_a, outs[0], outs[1]).astype(BF16)
    lse_ref[0] = lse_tile


def _band_attention(q, k, v, bsz, seq, dil):
    sub_len = seq // dil
    q_blk = min(256, sub_len)
    win = min(sub_len, q_blk + 2 * BAND)
    width = ATTN_WIDTH
    view = lambda a, w: a.reshape(bsz, sub_len, dil * w)
    qmap = lambda b, r, i: (b, i, r)
    kmap = lambda b, r, i: (b, 0, r)
    o, lse = pl.pallas_call(
        functools.partial(_band_attn_kernel, sub_len=sub_len, q_blk=q_blk, win=win),
        grid=(bsz, dil, sub_len // q_blk),
        in_specs=[pl.BlockSpec((1, q_blk, width), qmap),
                  pl.BlockSpec((1, sub_len, width), kmap),
                  pl.BlockSpec((1, sub_len, width), kmap)],
        out_specs=[pl.BlockSpec((1, q_blk, width), qmap),
                   pl.BlockSpec((1, q_blk, LANES), qmap)],
        out_shape=[jax.ShapeDtypeStruct((bsz, sub_len, dil * width), BF16),
                   jax.ShapeDtypeStruct((bsz, sub_len, dil * LANES), F32)],
        compiler_params=pltpu.CompilerParams(
            dimension_semantics=("parallel", "parallel", "arbitrary"),
            vmem_limit_bytes=VMEM_LIMIT),
        name=f"band_attention_d{dil}",
    )(view(q, width), view(k, width), view(v, width))
    return o.reshape(bsz * seq, width), lse.reshape(bsz * seq, LANES)


def _memkv_kernel(m_ref, g_ref, wk_ref, wv_ref, k_ref, v_ref):
    h = _rms(m_ref[0], g_ref[...]).astype(BF16)
    k_ref[0] = _dot(h, wk_ref[...]).astype(BF16)
    v_ref[0] = _dot(h, wv_ref[...]).astype(BF16)


def _mem_kv(mem, g, wk, wv):
    bsz, m, d = mem.shape
    blk = pl.BlockSpec((1, m, d), lambda i: (i, 0, 0))
    const = lambda i: (0, 0)
    return pl.pallas_call(
        _memkv_kernel,
        grid=(bsz,),
        in_specs=[blk, pl.BlockSpec((1, d), const), pl.BlockSpec((d, d), const),
                  pl.BlockSpec((d, d), const)],
        out_specs=[blk, blk],
        out_shape=[jax.ShapeDtypeStruct((bsz, m, d), BF16)] * 2,
        compiler_params=pltpu.CompilerParams(dimension_semantics=("parallel",)),
        name="mem_kv",
    )(mem, g, wk, wv)


def _post_kernel(o1_ref, o2_ref, o3_ref, l1_ref, l2_ref, l3_ref, mc_ref, x_ref,
                 ag_ref, wout_ref, xg_ref, wq_ref, kx_ref, vx_ref, wo_ref,
                 mg_ref, wr_ref, br_ref,
                 x2_ref, h3_ref, route_ref, cnt_ref,
                 mixed_ref, ox_ref, run_ref):
    tm = x_ref.shape[0]
    d = x_ref.shape[1]

    @pl.when(pl.program_id(0) == 0)
    def _():
        run_ref[...] = jnp.zeros_like(run_ref)

    lane = lax.broadcasted_iota(jnp.int32, (tm, LANES), 1)
    is_a = lane < HEAD_DIM

    pieces = []
    ssq = jnp.zeros((tm, 1), F32)
    for hp in range(ATTN_WIDTH // LANES):
        sl = slice(hp * LANES, (hp + 1) * LANES)

        def expand(l_ref):
            lt = l_ref[...]
            return jnp.where(is_a, lt[:, 2 * hp:2 * hp + 1], lt[:, 2 * hp + 1:2 * hp + 2])

        e1, e2, e3 = expand(l1_ref), expand(l2_ref), expand(l3_ref)
        mx = jnp.maximum(jnp.maximum(e1, e2), e3)
        w1, w2, w3 = jnp.exp(e1 - mx), jnp.exp(e2 - mx), jnp.exp(e3 - mx)
        num = (w1 * o1_ref[:, sl].astype(F32) + w2 * o2_ref[:, sl].astype(F32)
               + w3 * o3_ref[:, sl].astype(F32))
        a = num / (w1 + w2 + w3)
        pieces.append(a)
        ssq = ssq + jnp.sum(a * a, axis=1, keepdims=True)
    inv = lax.rsqrt(ssq * (1.0 / ATTN_WIDTH) + NORM_EPS)
    mixed_ref[:, 0:CONV_CH] = mc_ref[...]
    for hp, a in enumerate(pieces):
        sl = slice(hp * LANES, (hp + 1) * LANES)
        mixed_ref[:, CONV_CH + hp * LANES:CONV_CH + (hp + 1) * LANES] = (
            a * inv * ag_ref[:, sl]).astype(BF16)
    x1 = x_ref[...] + _dot(mixed_ref[...], wout_ref[...])

    hd = d // MEM_HEADS
    qx = (_dot(_rms(x1, xg_ref[...]).astype(BF16), wq_ref[...]) * (hd ** -0.5)).astype(BF16)
    for h in range(MEM_HEADS):
        sl = slice(h * hd, (h + 1) * hd)
        s = _dot_nt(qx[:, sl], kx_ref[0, :, sl])
        m = jnp.max(s, axis=1, keepdims=True)
        p = jnp.exp(s - m)
        l = jnp.sum(p, axis=1, keepdims=True)
        ox_ref[:, sl] = (_dot(p.astype(BF16), vx_ref[0, :, sl]) / l).astype(BF16)
    x2 = x1 + _dot(ox_ref[...], wo_ref[...])
    x2_ref[...] = x2

    h3 = _rms(x2, mg_ref[...])
    h3_ref[...] = h3
    h_hi = h3.astype(BF16)
    h_lo = (h3 - h_hi.astype(F32)).astype(BF16)
    w = wr_ref[...]
    w_hi = w.astype(BF16)
    w_lo = (w - w_hi.astype(F32)).astype(BF16)
    lg = _dot(h_hi, w_hi) + (_dot(h_hi, w_lo) + _dot(h_lo, w_hi)) + br_ref[...]

    def first_max(vals):
        vmax = jnp.max(vals, axis=1, keepdims=True)
        idx = jnp.min(jnp.where(vals == vmax, lane, LANES), axis=1, keepdims=True)
        return vmax, idx

    is_group = lane < N_GROUPS
    gmax, gidx = first_max(jnp.where(is_group, lg, NEG))
    p_g = 1.0 / jnp.sum(jnp.where(is_group, jnp.exp(lg - gmax), 0.0), axis=1, keepdims=True)
    lo = N_GROUPS + EXPERTS_PER_GROUP * gidx
    el = jnp.where((lane >= lo) & (lane < lo + EXPERTS_PER_GROUP), lg, NEG)
    v1, i1 = first_max(el)
    v2, i2 = first_max(jnp.where(lane == i1, NEG, el))
    t2 = jnp.exp(v2 - v1)
    g1 = p_g / (1.0 + t2)
    g2 = g1 * t2
    e1 = i1 - N_GROUPS
    e2 = i2 - N_GROUPS

    oh1 = lane == e1
    oh2 = lane == e2
    rr = lax.broadcasted_iota(jnp.int32, (tm, tm), 0)
    cc = lax.broadcasted_iota(jnp.int32, (tm, tm), 1)
    ltri = jnp.where(cc < rr, 1.0, 0.0).astype(BF16)
    oh1f = jnp.where(oh1, 1.0, 0.0)
    oh2f = jnp.where(oh2, 1.0, 0.0)
    pre1 = _dot(ltri, oh1f.astype(BF16))
    pre2 = _dot(ltri, oh2f.astype(BF16))
    tot1 = jnp.sum(oh1f, axis=0, keepdims=True)
    tot2 = jnp.sum(oh2f, axis=0, keepdims=True)
    run = run_ref[...]
    r1 = jnp.sum(oh1f * (run + pre1), axis=1, keepdims=True)
    r2 = jnp.sum(oh2f * (run + tot1 + pre2), axis=1, keepdims=True)
    run = run + tot1 + tot2
    run_ref[...] = run
    cnt_ref[...] = run

    route = jnp.zeros((tm, LANES), F32)
    for pos, val in enumerate((e1.astype(F32), e2.astype(F32), g1, g2, r1, r2)):
        route = jnp.where(lane == pos, val, route)
    route_ref[...] = route


def _post_mix(o_list, lse_list, mixc, xf, ag, w_out, xg, w_xq, kx, vx, w_xo, mg, w_rt, b_rt, seq):
    t, d = xf.shape
    tm = TM_POST
    blk_per_seq = seq // tm
    row = lambda i: (i, 0)
    const = lambda i: (0, 0)
    bmap = lambda i: (i // blk_per_seq, 0, 0)
    mlen = kx.shape[1]
    wide = pl.BlockSpec((tm, ATTN_WIDTH), row)
    narrow = pl.BlockSpec((tm, LANES), row)
    full = pl.BlockSpec((tm, d), row)
    mat = pl.BlockSpec((d, d), const)
    return pl.pallas_call(
        _post_kernel,
        grid=(t // tm,),
        in_specs=[wide, wide, wide, narrow, narrow, narrow,
                  pl.BlockSpec((tm, CONV_CH), row), full,
                  pl.BlockSpec((1, ATTN_WIDTH), const), mat,
                  pl.BlockSpec((1, d), const), mat,
                  pl.BlockSpec((1, mlen, d), bmap), pl.BlockSpec((1, mlen, d), bmap), mat,
                  pl.BlockSpec((1, d), const), pl.BlockSpec((d, LANES), const),
                  pl.BlockSpec((1, LANES), const)],
        out_specs=[full, full, narrow, pl.BlockSpec((1, LANES), const)],
        out_shape=[jax.ShapeDtypeStruct((t, d), F32), jax.ShapeDtypeStruct((t, d), F32),
                   jax.ShapeDtypeStruct((t, LANES), F32), jax.ShapeDtypeStruct((1, LANES), F32)],
        scratch_shapes=[pltpu.VMEM((tm, d), BF16), pltpu.VMEM((tm, d), BF16),
                        pltpu.VMEM((1, LANES), F32)],
        compiler_params=pltpu.CompilerParams(
            dimension_semantics=("arbitrary",), vmem_limit_bytes=VMEM_LIMIT),
        name="post_mix",
    )(*o_list, *lse_list, mixc, xf, ag, w_out, xg, w_xq, kx, vx, w_xo, mg, w_rt, b_rt)


def _dispatch_kernel(ps_ref, idx_ref, h_ref, xin_ref, xout_ref, sem):
    del xin_ref
    tm = h_ref.shape[0]

    def body(t, carry):
        for k in range(2):
            e = idx_ref[0, 0, k * tm + t]
            r = idx_ref[0, 0, (2 + k) * tm + t]
            pltpu.make_async_copy(h_ref.at[pl.ds(t, 1)], xout_ref.at[pl.ds(ps_ref[e] + r, 1)],
                                  sem).start()
        return carry

    lax.fori_loop(0, tm, body, 0)
    for _ in range(2):
        pltpu.make_async_copy(h_ref, xout_ref.at[pl.ds(0, tm)], sem).wait()


def _dispatch(pstarts, idx, h3, n_rows):
    t, d = h3.shape
    tm = TM_ROW
    x_rows = jnp.zeros((n_rows, d), F32)
    return pl.pallas_call(
        _dispatch_kernel,
        grid_spec=pltpu.PrefetchScalarGridSpec(
            num_scalar_prefetch=1,
            grid=(t // tm,),
            in_specs=[pl.BlockSpec((1, 1, 4 * tm), lambda i, ps: (i, 0, 0), memory_space=pltpu.SMEM),
                      pl.BlockSpec((tm, d), lambda i, ps: (i, 0)),
                      pl.BlockSpec(memory_space=pl.ANY)],
            out_specs=pl.BlockSpec(memory_space=pl.ANY),
            scratch_shapes=[pltpu.SemaphoreType.DMA(())]),
        out_shape=jax.ShapeDtypeStruct((n_rows, d), F32),
        input_output_aliases={3: 0},
        compiler_params=pltpu.CompilerParams(dimension_semantics=("arbitrary",)),
        name="moe_dispatch",
    )(pstarts, idx, h3, x_rows)


def _expert_kernel(be_ref, x_ref, w1_ref, w3_ref, w2_ref, y_ref, w1b, w3b, w2b):
    i = pl.program_id(0)
    prev = be_ref[jnp.maximum(i - 1, 0)]

    @pl.when((i == 0) | (be_ref[i] != prev))
    def _():
        w1b[...] = w1_ref[0].astype(BF16)
        w3b[...] = w3_ref[0].astype(BF16)
        w2b[...] = w2_ref[0].astype(BF16)

    xb = x_ref[...].astype(BF16)
    a = _dot(xb, w1b[...])
    g = _dot(xb, w3b[...])
    act = (a / (1.0 + jnp.exp(-a)) * g).astype(BF16)
    y_ref[...] = _dot(act, w2b[...])


def _experts(blk_e, x_rows, w1, w3, w2):
    p, d = x_rows.shape
    ff = w1.shape[2]
    bm = BM_EXPERT
    return pl.pallas_call(
        _expert_kernel,
        grid_spec=pltpu.PrefetchScalarGridSpec(
            num_scalar_prefetch=1,
            grid=(p // bm,),
            in_specs=[pl.BlockSpec((bm, d), lambda i, be: (i, 0)),
                      pl.BlockSpec((1, d, ff), lambda i, be: (be[i], 0, 0)),
                      pl.BlockSpec((1, d, ff), lambda i, be: (be[i], 0, 0)),
                      pl.BlockSpec((1, ff, d), lambda i, be: (be[i], 0, 0))],
            out_specs=pl.BlockSpec((bm, d), lambda i, be: (i, 0)),
            scratch_shapes=[pltpu.VMEM((d, ff), BF16), pltpu.VMEM((d, ff), BF16),
                            pltpu.VMEM((ff, d), BF16)]),
        out_shape=jax.ShapeDtypeStruct((p, d), F32),
        compiler_params=pltpu.CompilerParams(
            dimension_semantics=("arbitrary",), vmem_limit_bytes=VMEM_LIMIT),
        name="moe_experts",
    )(blk_e, x_rows, w1, w3, w2)


def _combine_kernel(ps_ref, idx_ref, x2_ref, route_ref, g_ref, y_ref, o_ref, buf0, buf1, sem):
    tm = x2_ref.shape[0]
    bufs = (buf0, buf1)

    def body(t, carry):
        for k in range(2):
            e = idx_ref[0, 0, k * tm + t]
            r = idx_ref[0, 0, (2 + k) * tm + t]
            pltpu.make_async_copy(y_ref.at[pl.ds(ps_ref[e] + r, 1)], bufs[k].at[pl.ds(t, 1)],
                                  sem).start()
        return carry

    lax.fori_loop(0, tm, body, 0)
    for k in range(2):
        pltpu.make_async_copy(y_ref.at[pl.ds(0, tm)], bufs[k], sem).wait()
    route = route_ref[...]
    x3 = x2_ref[...] + route[:, 2:3] * buf0[...] + route[:, 3:4] * buf1[...]
    o_ref[...] = _rms(x3, g_ref[...])


def _combine(pstarts, idx, x2, route, g, y_rows):
    t, d = x2.shape
    tm = TM_ROW
    row = lambda i, ps: (i, 0)
    return pl.pallas_call(
        _combine_kernel,
        grid_spec=pltpu.PrefetchScalarGridSpec(
            num_scalar_prefetch=1,
            grid=(t // tm,),
            in_specs=[pl.BlockSpec((1, 1, 4 * tm), lambda i, ps: (i, 0, 0), memory_space=pltpu.SMEM),
                      pl.BlockSpec((tm, d), row),
                      pl.BlockSpec((tm, LANES), row),
                      pl.BlockSpec((1, d), lambda i, ps: (0, 0)),
                      pl.BlockSpec(memory_space=pl.ANY)],
            out_specs=pl.BlockSpec((tm, d), row),
            scratch_shapes=[pltpu.VMEM((tm, d), F32), pltpu.VMEM((tm, d), F32),
                            pltpu.SemaphoreType.DMA(())]),
        out_shape=jax.ShapeDtypeStruct((t, d), F32),
        compiler_params=pltpu.CompilerParams(dimension_semantics=("arbitrary",)),
        name="moe_combine",
    )(pstarts, idx, x2, route, g, y_rows)


def kernel(x, mem, positions, mix_norm_g, w_in, conv_dw_w, conv_dw_b, conv_ln_g, conv_ln_b,
           conv_out_g, attn_out_g, w_out, xattn_norm_g, mem_norm_g, w_xq, w_xk, w_xv, w_xo,
           moe_norm_g, w_group, b_group, w_router, b_router, w1, w3, w2, final_norm_g):
    bsz, seq, d = x.shape
    assert w_in.shape[0] == 1, "single-layer encoder only"
    l = 0
    t = bsz * seq
    vec = lambda a: a.reshape(1, -1)
    cos_t, sin_t = _rope_tables(positions)
    xf = x.reshape(t, d)
    c, q, k, v = _in_projection(xf, vec(mix_norm_g[l]), w_in[l].astype(BF16), cos_t, sin_t, seq)
    mixc = _conformer_conv(c.reshape(bsz, seq, CONV_CH), conv_dw_w[l], vec(conv_dw_b[l]),
                           vec(conv_ln_g[l]), vec(conv_ln_b[l]), vec(conv_out_g[l]))
    branches = [_band_attention(q, k, v, bsz, seq, dil) for dil in DILATIONS]
    kx, vx = _mem_kv(mem, vec(mem_norm_g[l]), w_xk[l].astype(BF16), w_xv[l].astype(BF16))
    pad = LANES - N_GROUPS - N_EXPERTS
    w_rt = jnp.pad(jnp.concatenate([w_group[l], w_router[l]], axis=1), ((0, 0), (0, pad)))
    b_rt = jnp.pad(jnp.concatenate([b_group[l], b_router[l]]), (0, pad)).reshape(1, LANES)
    x2, h3, route, cnt = _post_mix(
        [o for o, _ in branches], [s for _, s in branches], mixc.reshape(t, CONV_CH), xf,
        vec(attn_out_g[l]), w_out[l].astype(BF16), vec(xattn_norm_g[l]), w_xq[l].astype(BF16),
        kx, vx, w_xo[l].astype(BF16), vec(moe_norm_g[l]), w_rt, b_rt, seq)

    bm = BM_EXPERT
    counts = cnt[0, :N_EXPERTS].astype(jnp.int32)
    pcounts = (counts + bm - 1) // bm * bm
    pends = jnp.cumsum(pcounts)
    pstarts = (pends - pcounts).astype(jnp.int32)
    n_blk = (2 * t) // bm + N_EXPERTS
    blk_e = jnp.minimum(
        jnp.searchsorted(pends, jnp.arange(n_blk, dtype=jnp.int32) * bm, side="right"),
        N_EXPERTS - 1).astype(jnp.int32)
    tr = TM_ROW
    idx = jnp.concatenate([route[:, 0:2], route[:, 4:6]], axis=1).astype(jnp.int32)
    idx = idx.reshape(t // tr, tr, 4).transpose(0, 2, 1).reshape(t // tr, 1, 4 * tr)

    x_rows = _dispatch(pstarts, idx, h3, n_blk * bm)
    y_rows = _experts(blk_e, x_rows, w1[l], w3[l], w2[l])
    out = _combine(pstarts, idx, x2, route, vec(final_norm_g), y_rows)
    return out.reshape(bsz, seq, d)
```

```python
import functools

import jax
import jax.numpy as jnp
from jax import lax
from jax.experimental import pallas as pl
from jax.experimental.pallas import tpu as pltpu

HEAD_DIM = 64
CONV_CH = 256
ATTN_HEADS = 12
ATTN_WIDTH = ATTN_HEADS * HEAD_DIM
CONV_KERNEL = 31
CONV_PAD = 16
BAND = 64
DILATIONS = (1, 4, 16)
ROPE_THETA = 10000.0
MEM_HEADS = 4
N_GROUPS = 4
EXPERTS_PER_GROUP = 8
N_EXPERTS = N_GROUPS * EXPERTS_PER_GROUP
NORM_EPS = 1e-6
LN_EPS = 1e-5
LANES = 128
NEG = -1e30

TM_IN = 512
TM_POST = 256
TM_ROW = 256
BM_EXPERT = 256
CONV_CHUNK = 128
VMEM_LIMIT = 56 * 1024 * 1024

F32 = jnp.float32
BF16 = jnp.bfloat16


def _rms(x, g):
    return x * lax.rsqrt(jnp.mean(x * x, axis=-1, keepdims=True) + NORM_EPS) * g


def _dot(a, b):
    return jnp.dot(a, b, preferred_element_type=F32)


def _dot_nt(a, b):
    return lax.dot_general(a, b, (((1,), (1,)), ((), ())), preferred_element_type=F32)


def _rope_kernel(pos_ref, inv_ref, sign_ref, cos_ref, sin_ref):
    ang = pos_ref[...].astype(F32) * inv_ref[...]
    cos_ref[...] = jnp.cos(ang)
    sin_ref[...] = jnp.sin(ang) * sign_ref[...]


def _rope_tables(positions):
    s = positions.shape[0]
    half = HEAD_DIM // 2
    inv = 1.0 / (ROPE_THETA ** (jnp.arange(half, dtype=F32) * (2.0 / HEAD_DIM)))
    inv = jnp.tile(inv, LANES // half)[None, :]
    sign = jnp.tile(jnp.concatenate([-jnp.ones((half,), F32), jnp.ones((half,), F32)]),
                    LANES // HEAD_DIM)[None, :]
    return pl.pallas_call(
        _rope_kernel,
        out_shape=(jax.ShapeDtypeStruct((s, LANES), F32), jax.ShapeDtypeStruct((s, LANES), F32)),
        name="rope_tables",
    )(positions.reshape(s, 1), inv, sign)


def _inproj_kernel(x_ref, g_ref, w_ref, cos_ref, sin_ref, c_ref, *rest):
    out_refs, slab_ref = rest[:-1], rest[-1]
    tm = x_ref.shape[0]
    n_grp = ATTN_WIDTH // LANES
    h = _rms(x_ref[...], g_ref[...]).astype(BF16)
    u = _dot(h, w_ref[:, 0:2 * CONV_CH])
    c_ref[...] = u[:, :CONV_CH] / (1.0 + jnp.exp(-u[:, CONV_CH:]))
    cos = cos_ref[...]
    sin = sin_ref[...]
    lane = lax.broadcasted_iota(jnp.int32, cos.shape, 1)
    first_half = (lane % HEAD_DIM) < (HEAD_DIM // 2)
    off = 2 * CONV_CH
    for which, scale in enumerate((HEAD_DIM ** -0.5, 1.0, None)):
        u = _dot(h, w_ref[:, off:off + ATTN_WIDTH])
        off += ATTN_WIDTH
        for j in range(n_grp):
            xs = u[:, j * LANES:(j + 1) * LANES]
            if scale is not None:
                partner = jnp.where(first_half, pltpu.roll(xs, LANES - 32, 1),
                                    pltpu.roll(xs, 32, 1))
                xs = (xs * cos + partner * sin) * scale
            slab_ref[which * n_grp + j] = xs
        for di, dil in enumerate(DILATIONS):
            ref = out_refs[3 * di + which]
            n = tm // dil
            for r in range(dil):
                for j in range(n_grp):
                    idx = which * n_grp + j
                    rows = slab_ref[idx] if dil == 1 else slab_ref[idx, pl.ds(r, n, stride=dil), :]
                    ref[0, r, :, j * LANES:(j + 1) * LANES] = rows.astype(BF16)


def _in_projection(xf, g, w_in, cos_t, sin_t, bsz, seq):
    t, d = xf.shape
    tm = TM_IN
    n_pos_blk = seq // tm
    row = lambda i: (i, 0)
    const = lambda i: (0, 0)
    cls = lambda i: (i // n_pos_blk, 0, i % n_pos_blk, 0)
    out_specs = [pl.BlockSpec((tm, CONV_CH), row)]
    out_shape = [jax.ShapeDtypeStruct((t, CONV_CH), F32)]
    for dil in DILATIONS:
        for _ in range(3):
            out_specs.append(pl.BlockSpec((1, dil, tm // dil, ATTN_WIDTH), cls))
            out_shape.append(jax.ShapeDtypeStruct((bsz, dil, seq // dil, ATTN_WIDTH), BF16))
    return pl.pallas_call(
        _inproj_kernel,
        grid=(t // tm,),
        in_specs=[
            pl.BlockSpec((tm, d), row),
            pl.BlockSpec((1, d), const),
            pl.BlockSpec(w_in.shape, const),
            pl.BlockSpec((tm, LANES), lambda i: (i % n_pos_blk, 0)),
            pl.BlockSpec((tm, LANES), lambda i: (i % n_pos_blk, 0)),
        ],
        out_specs=out_specs,
        out_shape=out_shape,
        scratch_shapes=[pltpu.VMEM((3 * ATTN_WIDTH // LANES, tm, LANES), F32)],
        compiler_params=pltpu.CompilerParams(
            dimension_semantics=("parallel",), vmem_limit_bytes=VMEM_LIMIT),
        name="in_projection",
    )(xf, g, w_in, cos_t, sin_t)


def _conv_kernel(c_ref, w_ref, b_ref, lng_ref, lnb_ref, og_ref, o_ref, pad_ref):
    seq = c_ref.shape[1]
    zeros = jnp.zeros((CONV_PAD, CONV_CH), F32)
    pad_ref[0:CONV_PAD, :] = zeros
    pad_ref[seq + CONV_PAD:seq + 2 * CONV_PAD, :] = zeros
    pad_ref[CONV_PAD:seq + CONV_PAD, :] = c_ref[0]
    shift = CONV_PAD - CONV_KERNEL // 2

    def body(i, carry):
        base = pl.multiple_of(i * CONV_CHUNK, CONV_CHUNK)
        win = pad_ref[pl.ds(base, CONV_CHUNK + 2 * CONV_PAD), :]
        acc = jnp.zeros((CONV_CHUNK, CONV_CH), F32)
        for b in range(8):
            taps = [k for k in range(CONV_KERNEL) if k % 8 == b]
            span = CONV_CHUNK + 8 * (len(taps) - 1)
            sb = win[b + shift:b + shift + span, :]
            for a, k in enumerate(taps):
                acc = acc + sb[8 * a:8 * a + CONV_CHUNK, :] * w_ref[k:k + 1, :]
        acc = acc + b_ref[...]
        mu = jnp.mean(acc, axis=-1, keepdims=True)
        cen = acc - mu
        var = jnp.mean(cen * cen, axis=-1, keepdims=True)
        y = cen * lax.rsqrt(var + LN_EPS) * lng_ref[...] + lnb_ref[...]
        y = y / (1.0 + jnp.exp(-y))
        o_ref[0, pl.ds(base, CONV_CHUNK), :] = _rms(y, og_ref[...]).astype(BF16)
        return carry

    lax.fori_loop(0, seq // CONV_CHUNK, body, 0)


def _conformer_conv(c, w, b, lng, lnb, og):
    bsz, seq, ch = c.shape
    vec = pl.BlockSpec((1, ch), lambda i: (0, 0))
    return pl.pallas_call(
        _conv_kernel,
        grid=(bsz,),
        in_specs=[pl.BlockSpec((1, seq, ch), lambda i: (i, 0, 0)),
                  pl.BlockSpec(w.shape, lambda i: (0, 0)), vec, vec, vec, vec],
        out_specs=pl.BlockSpec((1, seq, ch), lambda i: (i, 0, 0)),
        out_shape=jax.ShapeDtypeStruct((bsz, seq, ch), BF16),
        scratch_shapes=[pltpu.VMEM((seq + 2 * CONV_PAD, ch), F32)],
        compiler_params=pltpu.CompilerParams(
            dimension_semantics=("parallel",), vmem_limit_bytes=VMEM_LIMIT),
        name="conformer_conv",
    )(c, w, b, lng, lnb, og)


def _band_attn_kernel(q_ref, k_ref, v_ref, o_ref, lse_ref, *, sub_len, q_blk, win):
    q0 = pl.program_id(2) * q_blk
    ws = pl.multiple_of(jnp.clip(q0 - BAND, 0, sub_len - win), BAND)
    row = lax.broadcasted_iota(jnp.int32, (q_blk, win), 0)
    col = lax.broadcasted_iota(jnp.int32, (q_blk, win), 1)
    bias = jnp.where(jnp.abs(row - col + (q0 - ws)) <= BAND, 0.0, NEG).astype(F32)
    lane = lax.broadcasted_iota(jnp.int32, (q_blk, LANES), 1)
    is_a = lane < HEAD_DIM
    lse_tile = jnp.zeros((q_blk, LANES), F32)
    for hp in range(ATTN_WIDTH // LANES):
        sl = slice(hp * LANES, (hp + 1) * LANES)
        qh = q_ref[:, sl]
        kh = k_ref[pl.ds(ws, win), sl]
        vh = v_ref[pl.ds(ws, win), sl]
        zero = jnp.zeros_like(qh)
        outs = []
        for head, qx in enumerate((jnp.where(is_a, qh, zero), jnp.where(is_a, zero, qh))):
            s = _dot_nt(qx, kh) + bias
            m = jnp.max(s, axis=1, keepdims=True)
            p = jnp.exp(s - m)
            l = jnp.sum(p, axis=1, keepdims=True)
            outs.append(_dot(p.astype(BF16), vh) / l)
            lse_tile = jnp.where(lane == 2 * hp + head, m + jnp.log(l), lse_tile)
        o_ref[:, sl] = jnp.where(is_a, outs[0], outs[1]).astype(BF16)
    lse_ref[...] = lse_tile


def _band_attention(q, k, v):
    bsz, dil, sub_len, width = q.shape
    q_blk = min(256, sub_len)
    win = min(sub_len, q_blk + 2 * BAND)
    qmap = lambda b, r, i: (b, r, i, 0)
    kmap = lambda b, r, i: (b, r, 0, 0)
    return pl.pallas_call(
        functools.partial(_band_attn_kernel, sub_len=sub_len, q_blk=q_blk, win=win),
        grid=(bsz, dil, sub_len // q_blk),
        in_specs=[pl.BlockSpec((None, None, q_blk, width), qmap),
                  pl.BlockSpec((None, None, sub_len, width), kmap),
                  pl.BlockSpec((None, None, sub_len, width), kmap)],
        out_specs=[pl.BlockSpec((None, None, q_blk, width), qmap),
                   pl.BlockSpec((None, None, q_blk, LANES), qmap)],
        out_shape=[jax.ShapeDtypeStruct((bsz, dil, sub_len, width), BF16),
                   jax.ShapeDtypeStruct((bsz, dil, sub_len, LANES), F32)],
        compiler_params=pltpu.CompilerParams(
            dimension_semantics=("parallel", "parallel", "arbitrary"),
            vmem_limit_bytes=VMEM_LIMIT),
        name=f"band_attention_d{dil}",
    )(q, k, v)


def _memkv_kernel(m_ref, g_ref, wk_ref, wv_ref, k_ref, v_ref):
    h = _rms(m_ref[0], g_ref[...]).astype(BF16)
    k_ref[0] = _dot(h, wk_ref[...]).astype(BF16)
    v_ref[0] = _dot(h, wv_ref[...]).astype(BF16)


def _mem_kv(mem, g, wk, wv):
    bsz, m, d = mem.shape
    blk = pl.BlockSpec((1, m, d), lambda i: (i, 0, 0))
    const = lambda i: (0, 0)
    return pl.pallas_call(
        _memkv_kernel,
        grid=(bsz,),
        in_specs=[blk, pl.BlockSpec((1, d), const), pl.BlockSpec((d, d), const),
                  pl.BlockSpec((d, d), const)],
        out_specs=[blk, blk],
        out_shape=[jax.ShapeDtypeStruct((bsz, m, d), BF16)] * 2,
        compiler_params=pltpu.CompilerParams(dimension_semantics=("parallel",)),
        name="mem_kv",
    )(mem, g, wk, wv)


def _post_kernel(o1_ref, o2_ref, o3_ref, l1_ref, l2_ref, l3_ref, mc_ref, x_ref,
                 ag_ref, wout_ref, xg_ref, wq_ref, kx_ref, vx_ref, wo_ref,
                 mg_ref, wr_ref, br_ref,
                 x2_ref, h3_ref, route_ref, cnt_ref,
                 mixed_ref, ox_ref, run_ref, slab_ref):
    tm = x_ref.shape[0]
    d = x_ref.shape[1]
    n_grp = ATTN_WIDTH // LANES

    @pl.when(pl.program_id(0) == 0)
    def _():
        run_ref[...] = jnp.zeros_like(run_ref)

    lane = lax.broadcasted_iota(jnp.int32, (tm, LANES), 1)
    is_a = lane < HEAD_DIM

    for bi, (o_ref, l_ref) in enumerate(((o2_ref, l2_ref), (o3_ref, l3_ref))):
        dil = o_ref.shape[1]
        n = tm // dil
        base = bi * (n_grp + 1)
        for r in range(dil):
            rows = pl.ds(r, n, stride=dil)
            for j in range(n_grp):
                slab_ref[base + j, rows, :] = o_ref[0, r, :, j * LANES:(j + 1) * LANES].astype(F32)
            slab_ref[base + n_grp, rows, :] = l_ref[0, r]

    pieces = []
    ssq = jnp.zeros((tm, 1), F32)
    lse_tiles = (l1_ref[0, 0], slab_ref[n_grp], slab_ref[2 * n_grp + 1])
    for hp in range(n_grp):
        sl = slice(hp * LANES, (hp + 1) * LANES)

        def expand(lt):
            return jnp.where(is_a, lt[:, 2 * hp:2 * hp + 1], lt[:, 2 * hp + 1:2 * hp + 2])

        e1, e2, e3 = (expand(lt) for lt in lse_tiles)
        mx = jnp.maximum(jnp.maximum(e1, e2), e3)
        w1, w2, w3 = jnp.exp(e1 - mx), jnp.exp(e2 - mx), jnp.exp(e3 - mx)
        num = (w1 * o1_ref[0, 0, :, sl].astype(F32) + w2 * slab_ref[hp]
               + w3 * slab_ref[n_grp + 1 + hp])
        a = num / (w1 + w2 + w3)
        pieces.append(a)
        ssq = ssq + jnp.sum(a * a, axis=1, keepdims=True)
    inv = lax.rsqrt(ssq * (1.0 / ATTN_WIDTH) + NORM_EPS)
    mixed_ref[:, 0:CONV_CH] = mc_ref[...]
    for hp, a in enumerate(pieces):
        sl = slice(hp * LANES, (hp + 1) * LANES)
        mixed_ref[:, CONV_CH + hp * LANES:CONV_CH + (hp + 1) * LANES] = (
            a * inv * ag_ref[:, sl]).astype(BF16)
    x1 = x_ref[...] + _dot(mixed_ref[...], wout_ref[...])

    hd = d // MEM_HEADS
    qx = (_dot(_rms(x1, xg_ref[...]).astype(BF16), wq_ref[...]) * (hd ** -0.5)).astype(BF16)
    for h in range(MEM_HEADS):
        sl = slice(h * hd, (h + 1) * hd)
        s = _dot_nt(qx[:, sl], kx_ref[0, :, sl])
        m = jnp.max(s, axis=1, keepdims=True)
        p = jnp.exp(s - m)
        l = jnp.sum(p, axis=1, keepdims=True)
        ox_ref[:, sl] = (_dot(p.astype(BF16), vx_ref[0, :, sl]) / l).astype(BF16)
    x2 = x1 + _dot(ox_ref[...], wo_ref[...])
    x2_ref[...] = x2

    h3 = _rms(x2, mg_ref[...])
    h3_ref[...] = h3
    h_hi = h3.astype(BF16)
    h_lo = (h3 - h_hi.astype(F32)).astype(BF16)
    w = wr_ref[...]
    w_hi = w.astype(BF16)
    w_lo = (w - w_hi.astype(F32)).astype(BF16)
    lg = _dot(h_hi, w_hi) + (_dot(h_hi, w_lo) + _dot(h_lo, w_hi)) + br_ref[...]

    def first_max(vals):
        vmax = jnp.max(vals, axis=1, keepdims=True)
        idx = jnp.min(jnp.where(vals == vmax, lane, LANES), axis=1, keepdims=True)
        return vmax, idx

    is_group = lane < N_GROUPS
    gmax, gidx = first_max(jnp.where(is_group, lg, NEG))
    p_g = 1.0 / jnp.sum(jnp.where(is_group, jnp.exp(lg - gmax), 0.0), axis=1, keepdims=True)
    lo = N_GROUPS + EXPERTS_PER_GROUP * gidx
    el = jnp.where((lane >= lo) & (lane < lo + EXPERTS_PER_GROUP), lg, NEG)
    v1, i1 = first_max(el)
    v2, i2 = first_max(jnp.where(lane == i1, NEG, el))
    t2 = jnp.exp(v2 - v1)
    g1 = p_g / (1.0 + t2)
    g2 = g1 * t2
    e1 = i1 - N_GROUPS
    e2 = i2 - N_GROUPS

    oh1 = lane == e1
    oh2 = lane == e2
    rr = lax.broadcasted_iota(jnp.int32, (tm, tm), 0)
    cc = lax.broadcasted_iota(jnp.int32, (tm, tm), 1)
    ltri = jnp.where(cc < rr, 1.0, 0.0).astype(BF16)
    oh1f = jnp.where(oh1, 1.0, 0.0)
    oh2f = jnp.where(oh2, 1.0, 0.0)
    pre1 = _dot(ltri, oh1f.astype(BF16))
    pre2 = _dot(ltri, oh2f.astype(BF16))
    tot1 = jnp.sum(oh1f, axis=0, keepdims=True)
    tot2 = jnp.sum(oh2f, axis=0, keepdims=True)
    run = run_ref[...]
    r1 = jnp.sum(oh1f * (run + pre1), axis=1, keepdims=True)
    r2 = jnp.sum(oh2f * (run + tot1 + pre2), axis=1, keepdims=True)
    run = run + tot1 + tot2
    run_ref[...] = run
    cnt_ref[...] = run

    route = jnp.zeros((tm, LANES), F32)
    for pos, val in enumerate((e1.astype(F32), e2.astype(F32), g1, g2, r1, r2)):
        route = jnp.where(lane == pos, val, route)
    route_ref[...] = route


def _post_mix(o_list, lse_list, mixc, xf, ag, w_out, xg, w_xq, kx, vx, w_xo, mg, w_rt, b_rt, seq):
    t, d = xf.shape
    tm = TM_POST
    blk_per_seq = seq // tm
    row = lambda i: (i, 0)
    const = lambda i: (0, 0)
    bmap = lambda i: (i // blk_per_seq, 0, 0)
    mlen = kx.shape[1]
    cls = lambda i: (i // blk_per_seq, 0, i % blk_per_seq, 0)
    wide = [pl.BlockSpec((1, dil, tm // dil, ATTN_WIDTH), cls) for dil in DILATIONS]
    thin = [pl.BlockSpec((1, dil, tm // dil, LANES), cls) for dil in DILATIONS]
    narrow = pl.BlockSpec((tm, LANES), row)
    full = pl.BlockSpec((tm, d), row)
    mat = pl.BlockSpec((d, d), const)
    n_slab = 2 * (ATTN_WIDTH // LANES + 1)
    return pl.pallas_call(
        _post_kernel,
        grid=(t // tm,),
        in_specs=[*wide, *thin,
                  pl.BlockSpec((tm, CONV_CH), row), full,
                  pl.BlockSpec((1, ATTN_WIDTH), const), mat,
                  pl.BlockSpec((1, d), const), mat,
                  pl.BlockSpec((1, mlen, d), bmap), pl.BlockSpec((1, mlen, d), bmap), mat,
                  pl.BlockSpec((1, d), const), pl.BlockSpec((d, LANES), const),
                  pl.BlockSpec((1, LANES), const)],
        out_specs=[full, full, narrow, pl.BlockSpec((1, LANES), const)],
        out_shape=[jax.ShapeDtypeStruct((t, d), F32), jax.ShapeDtypeStruct((t, d), F32),
                   jax.ShapeDtypeStruct((t, LANES), F32), jax.ShapeDtypeStruct((1, LANES), F32)],
        scratch_shapes=[pltpu.VMEM((tm, d), BF16), pltpu.VMEM((tm, d), BF16),
                        pltpu.VMEM((1, LANES), F32), pltpu.VMEM((n_slab, tm, LANES), F32)],
        compiler_params=pltpu.CompilerParams(
            dimension_semantics=("arbitrary",), vmem_limit_bytes=VMEM_LIMIT),
        name="post_mix",
    )(*o_list, *lse_list, mixc, xf, ag, w_out, xg, w_xq, kx, vx, w_xo, mg, w_rt, b_rt)


def _dispatch_kernel(ps_ref, pe_ref, idx_ref, h_ref, xout_ref, zero_ref, sem):
    tm = h_ref.shape[0]
    bm = zero_ref.shape[0]

    @pl.when(pl.program_id(0) == 0)
    def _():
        zero_ref[...] = jnp.zeros_like(zero_ref)

        def pad_copy(e):
            last = pl.multiple_of(pe_ref[e] - bm, bm)
            return pltpu.make_async_copy(zero_ref, xout_ref.at[pl.ds(last, bm)], sem)

        def nonempty(e):
            return pe_ref[e] > ps_ref[e]

        def start(e, carry):
            pl.when(nonempty(e))(lambda: pad_copy(e).start())
            return carry

        def wait(e, carry):
            pl.when(nonempty(e))(lambda: pad_copy(e).wait())
            return carry

        lax.fori_loop(0, N_EXPERTS, start, 0)
        lax.fori_loop(0, N_EXPERTS, wait, 0)

        def tail_copy(b):
            return pltpu.make_async_copy(
                zero_ref, xout_ref.at[pl.ds(pl.multiple_of(b * bm, bm), bm)], sem)

        def tail_start(b, carry):
            tail_copy(b).start()
            return carry

        def tail_wait(b, carry):
            tail_copy(b).wait()
            return carry

        first_tail = pe_ref[N_EXPERTS - 1] // bm
        n_blk = xout_ref.shape[0] // bm
        lax.fori_loop(first_tail, n_blk, tail_start, 0)
        lax.fori_loop(first_tail, n_blk, tail_wait, 0)

    def body(t, carry):
        for k in range(2):
            e = idx_ref[0, 0, k * tm + t]
            r = idx_ref[0, 0, (2 + k) * tm + t]
            pltpu.make_async_copy(h_ref.at[pl.ds(t, 1)], xout_ref.at[pl.ds(ps_ref[e] + r, 1)],
                                  sem).start()
        return carry

    lax.fori_loop(0, tm, body, 0)
    for _ in range(2):
        pltpu.make_async_copy(h_ref, xout_ref.at[pl.ds(0, tm)], sem).wait()


def _dispatch(pstarts, pends, idx, h3, n_rows):
    t, d = h3.shape
    tm = TM_ROW
    return pl.pallas_call(
        _dispatch_kernel,
        grid_spec=pltpu.PrefetchScalarGridSpec(
            num_scalar_prefetch=2,
            grid=(t // tm,),
            in_specs=[pl.BlockSpec((1, 1, 4 * tm), lambda i, ps, pe: (i, 0, 0),
                                   memory_space=pltpu.SMEM),
                      pl.BlockSpec((tm, d), lambda i, ps, pe: (i, 0))],
            out_specs=pl.BlockSpec(memory_space=pl.ANY),
            scratch_shapes=[pltpu.VMEM((BM_EXPERT, d), F32), pltpu.SemaphoreType.DMA(())]),
        out_shape=jax.ShapeDtypeStruct((n_rows, d), F32),
        compiler_params=pltpu.CompilerParams(dimension_semantics=("arbitrary",)),
        name="moe_dispatch",
    )(pstarts, pends, idx, h3)


def _expert_kernel(be_ref, nu_ref, x_ref, w1_ref, w3_ref, w2_ref, y_ref, w1b, w3b, w2b):
    i = pl.program_id(0)
    prev = be_ref[jnp.maximum(i - 1, 0)]

    @pl.when((i == 0) | (be_ref[i] != prev))
    def _():
        w1b[...] = w1_ref[0].astype(BF16)
        w3b[...] = w3_ref[0].astype(BF16)
        w2b[...] = w2_ref[0].astype(BF16)

    @pl.when(i < nu_ref[0])
    def _():
        xb = x_ref[...].astype(BF16)
        a = _dot(xb, w1b[...])
        g = _dot(xb, w3b[...])
        act = (a / (1.0 + jnp.exp(-a)) * g).astype(BF16)
        y_ref[...] = _dot(act, w2b[...])

    @pl.when(i >= nu_ref[0])
    def _():
        y_ref[...] = jnp.zeros_like(y_ref)


def _experts(blk_e, n_used, x_rows, w1, w3, w2):
    p, d = x_rows.shape
    ff = w1.shape[2]
    bm = BM_EXPERT
    wmap = lambda i, be, nu: (be[i], 0, 0)
    return pl.pallas_call(
        _expert_kernel,
        grid_spec=pltpu.PrefetchScalarGridSpec(
            num_scalar_prefetch=2,
            grid=(p // bm,),
            in_specs=[pl.BlockSpec((bm, d), lambda i, be, nu: (jnp.minimum(i, nu[0] - 1), 0)),
                      pl.BlockSpec((1, d, ff), wmap),
                      pl.BlockSpec((1, d, ff), wmap),
                      pl.BlockSpec((1, ff, d), wmap)],
            out_specs=pl.BlockSpec((bm, d), lambda i, be, nu: (i, 0)),
            scratch_shapes=[pltpu.VMEM((d, ff), BF16), pltpu.VMEM((d, ff), BF16),
                            pltpu.VMEM((ff, d), BF16)]),
        out_shape=jax.ShapeDtypeStruct((p, d), F32),
        compiler_params=pltpu.CompilerParams(
            dimension_semantics=("arbitrary",), vmem_limit_bytes=VMEM_LIMIT),
        name="moe_experts",
    )(blk_e, n_used, x_rows, w1, w3, w2)


def _combine_kernel(ps_ref, idx_ref, x2_ref, route_ref, g_ref, y_ref, o_ref, buf0, buf1, sem):
    tm = x2_ref.shape[0]
    bufs = (buf0, buf1)

    def body(t, carry):
        for k in range(2):
            e = idx_ref[0, 0, k * tm + t]
            r = idx_ref[0, 0, (2 + k) * tm + t]
            pltpu.make_async_copy(y_ref.at[pl.ds(ps_ref[e] + r, 1)], bufs[k].at[pl.ds(t, 1)],
                                  sem).start()
        return carry

    lax.fori_loop(0, tm, body, 0)
    for k in range(2):
        pltpu.make_async_copy(y_ref.at[pl.ds(0, tm)], bufs[k], sem).wait()
    route = route_ref[...]
    x3 = x2_ref[...] + route[:, 2:3] * buf0[...] + route[:, 3:4] * buf1[...]
    o_ref[...] = _rms(x3, g_ref[...])


def _combine(pstarts, idx, x2, route, g, y_rows):
    t, d = x2.shape
    tm = TM_ROW
    row = lambda i, ps: (i, 0)
    return pl.pallas_call(
        _combine_kernel,
        grid_spec=pltpu.PrefetchScalarGridSpec(
            num_scalar_prefetch=1,
            grid=(t // tm,),
            in_specs=[pl.BlockSpec((1, 1, 4 * tm), lambda i, ps: (i, 0, 0), memory_space=pltpu.SMEM),
                      pl.BlockSpec((tm, d), row),
                      pl.BlockSpec((tm, LANES), row),
                      pl.BlockSpec((1, d), lambda i, ps: (0, 0)),
                      pl.BlockSpec(memory_space=pl.ANY)],
            out_specs=pl.BlockSpec((tm, d), row),
            scratch_shapes=[pltpu.VMEM((tm, d), F32), pltpu.VMEM((tm, d), F32),
                            pltpu.SemaphoreType.DMA(())]),
        out_shape=jax.ShapeDtypeStruct((t, d), F32),
        compiler_params=pltpu.CompilerParams(dimension_semantics=("arbitrary",)),
        name="moe_combine",
    )(pstarts, idx, x2, route, g, y_rows)


def kernel(x, mem, positions, mix_norm_g, w_in, conv_dw_w, conv_dw_b, conv_ln_g, conv_ln_b,
           conv_out_g, attn_out_g, w_out, xattn_norm_g, mem_norm_g, w_xq, w_xk, w_xv, w_xo,
           moe_norm_g, w_group, b_group, w_router, b_router, w1, w3, w2, final_norm_g):
    bsz, seq, d = x.shape
    assert w_in.shape[0] == 1, "single-layer encoder only"
    l = 0
    t = bsz * seq
    vec = lambda a: a.reshape(1, -1)
    cos_t, sin_t = _rope_tables(positions)
    xf = x.reshape(t, d)
    c, *qkv = _in_projection(xf, vec(mix_norm_g[l]), w_in[l].astype(BF16), cos_t, sin_t, bsz, seq)
    mixc = _conformer_conv(c.reshape(bsz, seq, CONV_CH), conv_dw_w[l], vec(conv_dw_b[l]),
                           vec(conv_ln_g[l]), vec(conv_ln_b[l]), vec(conv_out_g[l]))
    branches = [_band_attention(*qkv[3 * i:3 * i + 3]) for i in range(len(DILATIONS))]
    kx, vx = _mem_kv(mem, vec(mem_norm_g[l]), w_xk[l].astype(BF16), w_xv[l].astype(BF16))
    pad = LANES - N_GROUPS - N_EXPERTS
    w_rt = jnp.pad(jnp.concatenate([w_group[l], w_router[l]], axis=1), ((0, 0), (0, pad)))
    b_rt = jnp.pad(jnp.concatenate([b_group[l], b_router[l]]), (0, pad)).reshape(1, LANES)
    x2, h3, route, cnt = _post_mix(
        [o for o, _ in branches], [s for _, s in branches], mixc.reshape(t, CONV_CH), xf,
        vec(attn_out_g[l]), w_out[l].astype(BF16), vec(xattn_norm_g[l]), w_xq[l].astype(BF16),
        kx, vx, w_xo[l].astype(BF16), vec(moe_norm_g[l]), w_rt, b_rt, seq)

    bm = BM_EXPERT
    counts = cnt[0, :N_EXPERTS].astype(jnp.int32)
    pcounts = (counts + bm - 1) // bm * bm
    pends = jnp.cumsum(pcounts)
    pstarts = (pends - pcounts).astype(jnp.int32)
    pends = pends.astype(jnp.int32)
    n_blk = (2 * t) // bm + N_EXPERTS
    n_used = pends[-1:] // bm
    blk = jnp.minimum(jnp.arange(n_blk, dtype=jnp.int32), n_used - 1)
    blk_e = jnp.sum((blk[:, None] * bm >= pends[None, :]).astype(jnp.int32), axis=1)
    tr = TM_ROW
    idx = jnp.concatenate([route[:, 0:2], route[:, 4:6]], axis=1).astype(jnp.int32)
    idx = idx.reshape(t // tr, tr, 4).transpose(0, 2, 1).reshape(t // tr, 1, 4 * tr)

    x_rows = _dispatch(pstarts, pends, idx, h3, n_blk * bm)
    y_rows = _experts(blk_e, n_used, x_rows, w1[l], w3[l], w2[l])
    out = _combine(pstarts, idx, x2, route, vec(final_norm_g), y_rows)
    return out.reshape(bsz, seq, d)
```

```python
import functools

import jax
import jax.numpy as jnp
from jax import lax
from jax.experimental import pallas as pl
from jax.experimental.pallas import tpu as pltpu

HEAD_DIM = 64
CONV_CH = 256
ATTN_HEADS = 12
ATTN_WIDTH = ATTN_HEADS * HEAD_DIM
CONV_KERNEL = 31
CONV_PAD = 16
BAND = 64
DILATIONS = (1, 4, 16)
ROPE_THETA = 10000.0
MEM_HEADS = 4
N_GROUPS = 4
EXPERTS_PER_GROUP = 8
N_EXPERTS = N_GROUPS * EXPERTS_PER_GROUP
NORM_EPS = 1e-6
LN_EPS = 1e-5
LANES = 128
NEG = -1e30

TM_IN = 512
TM_POST = 512
POST_HALF = 256
TM_ROW = 256
BM_EXPERT = 256
CONV_CHUNK = 128
ATTN_SUB = 128
VMEM_LIMIT = 56 * 1024 * 1024

F32 = jnp.float32
BF16 = jnp.bfloat16


def _rms(x, g):
    return x * lax.rsqrt(jnp.mean(x * x, axis=-1, keepdims=True) + NORM_EPS) * g


def _dot(a, b):
    return jnp.dot(a, b, preferred_element_type=F32)


def _dot_nt(a, b):
    return lax.dot_general(a, b, (((1,), (1,)), ((), ())), preferred_element_type=F32)


def _rope_kernel(pos_ref, inv_ref, sign_ref, cos_ref, sin_ref):
    ang = pos_ref[...].astype(F32) * inv_ref[...]
    cos_ref[...] = jnp.cos(ang)
    sin_ref[...] = jnp.sin(ang) * sign_ref[...]


def _rope_tables(positions):
    s = positions.shape[0]
    half = HEAD_DIM // 2
    inv = 1.0 / (ROPE_THETA ** (jnp.arange(half, dtype=F32) * (2.0 / HEAD_DIM)))
    inv = jnp.tile(inv, LANES // half)[None, :]
    sign = jnp.tile(jnp.concatenate([-jnp.ones((half,), F32), jnp.ones((half,), F32)]),
                    LANES // HEAD_DIM)[None, :]
    return pl.pallas_call(
        _rope_kernel,
        out_shape=(jax.ShapeDtypeStruct((s, LANES), F32), jax.ShapeDtypeStruct((s, LANES), F32)),
        name="rope_tables",
    )(positions.reshape(s, 1), inv, sign)


def _inproj_kernel(x_ref, g_ref, w_ref, cos_ref, sin_ref, c_ref, *rest):
    out_refs, slab_ref = rest[:-1], rest[-1]
    tm = x_ref.shape[0]
    n_grp = ATTN_WIDTH // LANES
    h = _rms(x_ref[...], g_ref[...]).astype(BF16)
    u = _dot(h, w_ref[:, 0:2 * CONV_CH])
    c_ref[...] = u[:, :CONV_CH] / (1.0 + jnp.exp(-u[:, CONV_CH:]))
    cos = cos_ref[...]
    sin = sin_ref[...]
    lane = lax.broadcasted_iota(jnp.int32, cos.shape, 1)
    first_half = (lane % HEAD_DIM) < (HEAD_DIM // 2)
    off = 2 * CONV_CH
    for which, scale in enumerate((HEAD_DIM ** -0.5, 1.0, None)):
        u = _dot(h, w_ref[:, off:off + ATTN_WIDTH])
        off += ATTN_WIDTH
        for j in range(n_grp):
            xs = u[:, j * LANES:(j + 1) * LANES]
            if scale is not None:
                partner = jnp.where(first_half, pltpu.roll(xs, LANES - 32, 1),
                                    pltpu.roll(xs, 32, 1))
                xs = (xs * cos + partner * sin) * scale
            slab_ref[which * n_grp + j] = xs
        for di, dil in enumerate(DILATIONS):
            ref = out_refs[3 * di + which]
            n = tm // dil
            for r in range(dil):
                for j in range(n_grp):
                    idx = which * n_grp + j
                    rows = slab_ref[idx] if dil == 1 else slab_ref[idx, pl.ds(r, n, stride=dil), :]
                    ref[0, r, :, j * LANES:(j + 1) * LANES] = rows.astype(BF16)


def _in_projection(xf, g, w_in, cos_t, sin_t, bsz, seq):
    t, d = xf.shape
    tm = TM_IN
    n_pos_blk = seq // tm
    row = lambda i: (i, 0)
    const = lambda i: (0, 0)
    cls = lambda i: (i // n_pos_blk, 0, i % n_pos_blk, 0)
    out_specs = [pl.BlockSpec((tm, CONV_CH), row)]
    out_shape = [jax.ShapeDtypeStruct((t, CONV_CH), F32)]
    for dil in DILATIONS:
        for _ in range(3):
            out_specs.append(pl.BlockSpec((1, dil, tm // dil, ATTN_WIDTH), cls))
            out_shape.append(jax.ShapeDtypeStruct((bsz, dil, seq // dil, ATTN_WIDTH), BF16))
    return pl.pallas_call(
        _inproj_kernel,
        grid=(t // tm,),
        in_specs=[
            pl.BlockSpec((tm, d), row),
            pl.BlockSpec((1, d), const),
            pl.BlockSpec(w_in.shape, const),
            pl.BlockSpec((tm, LANES), lambda i: (i % n_pos_blk, 0)),
            pl.BlockSpec((tm, LANES), lambda i: (i % n_pos_blk, 0)),
        ],
        out_specs=out_specs,
        out_shape=out_shape,
        scratch_shapes=[pltpu.VMEM((3 * ATTN_WIDTH // LANES, tm, LANES), F32)],
        compiler_params=pltpu.CompilerParams(
            dimension_semantics=("parallel",), vmem_limit_bytes=VMEM_LIMIT),
        name="in_projection",
    )(xf, g, w_in, cos_t, sin_t)


def _conv_kernel(c_ref, w_ref, b_ref, lng_ref, lnb_ref, og_ref, o_ref, pad_ref):
    seq = c_ref.shape[1]
    zeros = jnp.zeros((CONV_PAD, CONV_CH), F32)
    pad_ref[0:CONV_PAD, :] = zeros
    pad_ref[seq + CONV_PAD:seq + 2 * CONV_PAD, :] = zeros
    pad_ref[CONV_PAD:seq + CONV_PAD, :] = c_ref[0]
    shift = CONV_PAD - CONV_KERNEL // 2

    def body(i, carry):
        base = pl.multiple_of(i * CONV_CHUNK, CONV_CHUNK)
        win = pad_ref[pl.ds(base, CONV_CHUNK + 2 * CONV_PAD), :]
        acc = jnp.zeros((CONV_CHUNK, CONV_CH), F32)
        for b in range(8):
            taps = [k for k in range(CONV_KERNEL) if k % 8 == b]
            span = CONV_CHUNK + 8 * (len(taps) - 1)
            sb = win[b + shift:b + shift + span, :]
            for a, k in enumerate(taps):
                acc = acc + sb[8 * a:8 * a + CONV_CHUNK, :] * w_ref[k:k + 1, :]
        acc = acc + b_ref[...]
        mu = jnp.mean(acc, axis=-1, keepdims=True)
        cen = acc - mu
        var = jnp.mean(cen * cen, axis=-1, keepdims=True)
        y = cen * lax.rsqrt(var + LN_EPS) * lng_ref[...] + lnb_ref[...]
        y = y / (1.0 + jnp.exp(-y))
        o_ref[0, pl.ds(base, CONV_CHUNK), :] = _rms(y, og_ref[...]).astype(BF16)
        return carry

    lax.fori_loop(0, seq // CONV_CHUNK, body, 0)


def _conformer_conv(c, w, b, lng, lnb, og):
    bsz, seq, ch = c.shape
    vec = pl.BlockSpec((1, ch), lambda i: (0, 0))
    return pl.pallas_call(
        _conv_kernel,
        grid=(bsz,),
        in_specs=[pl.BlockSpec((1, seq, ch), lambda i: (i, 0, 0)),
                  pl.BlockSpec(w.shape, lambda i: (0, 0)), vec, vec, vec, vec],
        out_specs=pl.BlockSpec((1, seq, ch), lambda i: (i, 0, 0)),
        out_shape=jax.ShapeDtypeStruct((bsz, seq, ch), BF16),
        scratch_shapes=[pltpu.VMEM((seq + 2 * CONV_PAD, ch), F32)],
        compiler_params=pltpu.CompilerParams(
            dimension_semantics=("parallel",), vmem_limit_bytes=VMEM_LIMIT),
        name="conformer_conv",
    )(c, w, b, lng, lnb, og)


def _band_attn_kernel(q_ref, k_ref, v_ref, o_ref, lse_ref, *, sub_len, q_blk, win):
    sq = ATTN_SUB
    row = lax.broadcasted_iota(jnp.int32, (2 * sq, win), 0) % sq
    col = lax.broadcasted_iota(jnp.int32, (2 * sq, win), 1)
    lane = lax.broadcasted_iota(jnp.int32, (sq, LANES), 1)
    is_a = lane < HEAD_DIM
    for sub in range(q_blk // sq):
        q0 = pl.program_id(2) * q_blk + sub * sq
        ws = pl.multiple_of(jnp.clip(q0 - BAND, 0, sub_len - win), BAND)
        bias = jnp.where(jnp.abs(row - col + (q0 - ws)) <= BAND, 0.0, NEG).astype(F32)
        rows = slice(sub * sq, (sub + 1) * sq)
        for hp in range(ATTN_WIDTH // LANES):
            sl = slice(hp * LANES, (hp + 1) * LANES)
            qh = q_ref[rows, sl]
            kh = k_ref[pl.ds(ws, win), sl]
            vh = v_ref[pl.ds(ws, win), sl]
            zero = jnp.zeros_like(qh)
            q2 = jnp.concatenate([jnp.where(is_a, qh, zero), jnp.where(is_a, zero, qh)], axis=0)
            s = _dot_nt(q2, kh) + bias
            m = jnp.max(s, axis=1, keepdims=True)
            p = jnp.exp(s - m)
            l = jnp.sum(p, axis=1, keepdims=True)
            o2 = _dot(p.astype(BF16), vh) / l
            lse2 = m + jnp.log(l)
            o_ref[rows, sl] = jnp.where(is_a, o2[:sq], o2[sq:]).astype(BF16)
            lse_ref[rows, sl] = jnp.where(is_a, lse2[:sq], lse2[sq:])


def _band_attention(q, k, v):
    bsz, dil, sub_len, width = q.shape
    q_blk = min(256, sub_len)
    win = min(sub_len, ATTN_SUB + 2 * BAND)
    qmap = lambda b, r, i: (b, r, i, 0)
    kmap = lambda b, r, i: (b, r, 0, 0)
    return pl.pallas_call(
        functools.partial(_band_attn_kernel, sub_len=sub_len, q_blk=q_blk, win=win),
        grid=(bsz, dil, sub_len // q_blk),
        in_specs=[pl.BlockSpec((None, None, q_blk, width), qmap),
                  pl.BlockSpec((None, None, sub_len, width), kmap),
                  pl.BlockSpec((None, None, sub_len, width), kmap)],
        out_specs=[pl.BlockSpec((None, None, q_blk, width), qmap),
                   pl.BlockSpec((None, None, q_blk, width), qmap)],
        out_shape=[jax.ShapeDtypeStruct((bsz, dil, sub_len, width), BF16),
                   jax.ShapeDtypeStruct((bsz, dil, sub_len, width), F32)],
        compiler_params=pltpu.CompilerParams(
            dimension_semantics=("parallel", "parallel", "arbitrary"),
            vmem_limit_bytes=VMEM_LIMIT),
        name=f"band_attention_d{dil}",
    )(q, k, v)


def _memkv_kernel(m_ref, g_ref, wk_ref, wv_ref, k_ref, v_ref):
    h = _rms(m_ref[0], g_ref[...]).astype(BF16)
    k_ref[0] = _dot(h, wk_ref[...]).astype(BF16)
    v_ref[0] = _dot(h, wv_ref[...]).astype(BF16)


def _mem_kv(mem, g, wk, wv):
    bsz, m, d = mem.shape
    blk = pl.BlockSpec((1, m, d), lambda i: (i, 0, 0))
    const = lambda i: (0, 0)
    return pl.pallas_call(
        _memkv_kernel,
        grid=(bsz,),
        in_specs=[blk, pl.BlockSpec((1, d), const), pl.BlockSpec((d, d), const),
                  pl.BlockSpec((d, d), const)],
        out_specs=[blk, blk],
        out_shape=[jax.ShapeDtypeStruct((bsz, m, d), BF16)] * 2,
        compiler_params=pltpu.CompilerParams(dimension_semantics=("parallel",)),
        name="mem_kv",
    )(mem, g, wk, wv)


def _post_kernel(o1_ref, o2_ref, o3_ref, l1_ref, l2_ref, l3_ref, mc_ref, x_ref,
                 ag_ref, wout_ref, xg_ref, wq_ref, kx_ref, vx_ref, wo_ref,
                 mg_ref, wr_ref, br_ref,
                 x2_ref, h3_ref, route_ref, cnt_ref,
                 run_ref, slab_ref, *scratch):
    tm = x_ref.shape[0]
    d = x_ref.shape[1]
    n_grp = ATTN_WIDTH // LANES

    @pl.when(pl.program_id(0) == 0)
    def _():
        run_ref[...] = jnp.zeros_like(run_ref)

    for bi, (o_ref, l_ref) in enumerate(((o2_ref, l2_ref), (o3_ref, l3_ref))):
        dil = o_ref.shape[1]
        n = tm // dil
        base = bi * 2 * n_grp
        for r in range(dil):
            rows = pl.ds(r, n, stride=dil)
            for j in range(n_grp):
                sl = slice(j * LANES, (j + 1) * LANES)
                slab_ref[base + j, rows, :] = o_ref[0, r, :, sl].astype(F32)
                slab_ref[base + n_grp + j, rows, :] = l_ref[0, r, :, sl]

    hm = POST_HALF
    lane = lax.broadcasted_iota(jnp.int32, (hm, LANES), 1)
    is_a = lane < HEAD_DIM
    rr = lax.broadcasted_iota(jnp.int32, (hm, hm), 0)
    cc = lax.broadcasted_iota(jnp.int32, (hm, hm), 1)
    ltri = jnp.where(cc < rr, 1.0, 0.0).astype(BF16)
    w_rt = wr_ref[...]
    w_hi = w_rt.astype(BF16)
    w_lo = (w_rt - w_hi.astype(F32)).astype(BF16)

    def first_max(vals):
        vmax = jnp.max(vals, axis=1, keepdims=True)
        idx = jnp.min(jnp.where(vals == vmax, lane, LANES), axis=1, keepdims=True)
        return vmax, idx

    run_box = [run_ref[...]]

    def chain(r0, mixed_ref, ox_ref):
        rs = slice(r0, r0 + hm)

        pieces = []
        ssq = jnp.zeros((hm, 1), F32)
        for hp in range(n_grp):
            sl = slice(hp * LANES, (hp + 1) * LANES)
            e1 = l1_ref[0, 0, rs, sl]
            e2 = slab_ref[n_grp + hp, rs, :]
            e3 = slab_ref[3 * n_grp + hp, rs, :]
            mx = jnp.maximum(jnp.maximum(e1, e2), e3)
            w1, w2, w3 = jnp.exp(e1 - mx), jnp.exp(e2 - mx), jnp.exp(e3 - mx)
            num = (w1 * o1_ref[0, 0, rs, sl].astype(F32) + w2 * slab_ref[hp, rs, :]
                   + w3 * slab_ref[2 * n_grp + hp, rs, :])
            a = num / (w1 + w2 + w3)
            pieces.append(a)
            ssq = ssq + jnp.sum(a * a, axis=1, keepdims=True)
        inv = lax.rsqrt(ssq * (1.0 / ATTN_WIDTH) + NORM_EPS)
        mixed_ref[:, 0:CONV_CH] = mc_ref[rs, :]
        for hp, a in enumerate(pieces):
            sl = slice(hp * LANES, (hp + 1) * LANES)
            mixed_ref[:, CONV_CH + hp * LANES:CONV_CH + (hp + 1) * LANES] = (
                a * inv * ag_ref[:, sl]).astype(BF16)
        yield
        x1 = x_ref[rs, :] + _dot(mixed_ref[...], wout_ref[...])
        yield

        hd = d // MEM_HEADS
        qx = (_dot(_rms(x1, xg_ref[...]).astype(BF16), wq_ref[...]) * (hd ** -0.5)).astype(BF16)
        yield
        for h in range(MEM_HEADS):
            sl = slice(h * hd, (h + 1) * hd)
            s = _dot_nt(qx[:, sl], kx_ref[0, :, sl])
            m = jnp.max(s, axis=1, keepdims=True)
            p = jnp.exp(s - m)
            l = jnp.sum(p, axis=1, keepdims=True)
            ox_ref[:, sl] = (_dot(p.astype(BF16), vx_ref[0, :, sl]) / l).astype(BF16)
        yield
        x2 = x1 + _dot(ox_ref[...], wo_ref[...])
        x2_ref[rs, :] = x2
        yield

        h3 = _rms(x2, mg_ref[...])
        h3_ref[rs, :] = h3
        h_hi = h3.astype(BF16)
        h_lo = (h3 - h_hi.astype(F32)).astype(BF16)
        lg = _dot(h_hi, w_hi) + (_dot(h_hi, w_lo) + _dot(h_lo, w_hi)) + br_ref[...]

        is_group = lane < N_GROUPS
        gmax, gidx = first_max(jnp.where(is_group, lg, NEG))
        p_g = 1.0 / jnp.sum(jnp.where(is_group, jnp.exp(lg - gmax), 0.0), axis=1, keepdims=True)
        lo = N_GROUPS + EXPERTS_PER_GROUP * gidx
        el = jnp.where((lane >= lo) & (lane < lo + EXPERTS_PER_GROUP), lg, NEG)
        v1, i1 = first_max(el)
        v2, i2 = first_max(jnp.where(lane == i1, NEG, el))
        t2 = jnp.exp(v2 - v1)
        g1 = p_g / (1.0 + t2)
        g2 = g1 * t2
        e1 = i1 - N_GROUPS
        e2 = i2 - N_GROUPS

        oh1f = jnp.where(lane == e1, 1.0, 0.0)
        oh2f = jnp.where(lane == e2, 1.0, 0.0)
        pre1 = _dot(ltri, oh1f.astype(BF16))
        pre2 = _dot(ltri, oh2f.astype(BF16))
        tot1 = jnp.sum(oh1f, axis=0, keepdims=True)
        tot2 = jnp.sum(oh2f, axis=0, keepdims=True)
        run = run_box[0]
        r1 = jnp.sum(oh1f * (run + pre1), axis=1, keepdims=True)
        r2 = jnp.sum(oh2f * (run + tot1 + pre2), axis=1, keepdims=True)
        run_box[0] = run + tot1 + tot2

        route = jnp.zeros((hm, LANES), F32)
        for pos, val in enumerate((e1.astype(F32), e2.astype(F32), g1, g2, r1, r2)):
            route = jnp.where(lane == pos, val, route)
        route_ref[rs, :] = route
        yield

    chains = [chain(i * hm, scratch[2 * i], scratch[2 * i + 1]) for i in range(tm // hm)]
    n_stage = 6
    for step in range(n_stage + len(chains) - 1):
        for ci, ch in enumerate(chains):
            if 0 <= step - ci < n_stage:
                next(ch)
    run_ref[...] = run_box[0]
    cnt_ref[...] = run_box[0]


def _post_mix(o_list, lse_list, mixc, xf, ag, w_out, xg, w_xq, kx, vx, w_xo, mg, w_rt, b_rt, seq):
    t, d = xf.shape
    tm = TM_POST
    blk_per_seq = seq // tm
    row = lambda i: (i, 0)
    const = lambda i: (0, 0)
    bmap = lambda i: (i // blk_per_seq, 0, 0)
    mlen = kx.shape[1]
    cls = lambda i: (i // blk_per_seq, 0, i % blk_per_seq, 0)
    wide = [pl.BlockSpec((1, dil, tm // dil, ATTN_WIDTH), cls) for dil in DILATIONS]
    narrow = pl.BlockSpec((tm, LANES), row)
    full = pl.BlockSpec((tm, d), row)
    mat = pl.BlockSpec((d, d), const)
    n_slab = 4 * (ATTN_WIDTH // LANES)
    return pl.pallas_call(
        _post_kernel,
        grid=(t // tm,),
        in_specs=[*wide, *wide,
                  pl.BlockSpec((tm, CONV_CH), row), full,
                  pl.BlockSpec((1, ATTN_WIDTH), const), mat,
                  pl.BlockSpec((1, d), const), mat,
                  pl.BlockSpec((1, mlen, d), bmap), pl.BlockSpec((1, mlen, d), bmap), mat,
                  pl.BlockSpec((1, d), const), pl.BlockSpec((d, LANES), const),
                  pl.BlockSpec((1, LANES), const)],
        out_specs=[full, full, narrow, pl.BlockSpec((1, LANES), const)],
        out_shape=[jax.ShapeDtypeStruct((t, d), F32), jax.ShapeDtypeStruct((t, d), F32),
                   jax.ShapeDtypeStruct((t, LANES), F32), jax.ShapeDtypeStruct((1, LANES), F32)],
        scratch_shapes=[pltpu.VMEM((1, LANES), F32), pltpu.VMEM((n_slab, tm, LANES), F32)]
        + [pltpu.VMEM((POST_HALF, d), BF16)] * (2 * (tm // POST_HALF)),
        compiler_params=pltpu.CompilerParams(
            dimension_semantics=("arbitrary",), vmem_limit_bytes=VMEM_LIMIT),
        name="post_mix",
    )(*o_list, *lse_list, mixc, xf, ag, w_out, xg, w_xq, kx, vx, w_xo, mg, w_rt, b_rt)


def _dispatch_kernel(ps_ref, pe_ref, idx_ref, h_ref, xout_ref, zero_ref, sem):
    tm = h_ref.shape[0]
    bm = zero_ref.shape[0]

    @pl.when(pl.program_id(0) == 0)
    def _():
        zero_ref[...] = jnp.zeros_like(zero_ref)

        def pad_copy(e):
            last = pl.multiple_of(pe_ref[e] - bm, bm)
            return pltpu.make_async_copy(zero_ref, xout_ref.at[pl.ds(last, bm)], sem)

        def nonempty(e):
            return pe_ref[e] > ps_ref[e]

        def start(e, carry):
            pl.when(nonempty(e))(lambda: pad_copy(e).start())
            return carry

        def wait(e, carry):
            pl.when(nonempty(e))(lambda: pad_copy(e).wait())
            return carry

        lax.fori_loop(0, N_EXPERTS, start, 0)
        lax.fori_loop(0, N_EXPERTS, wait, 0)

        def tail_copy(b):
            return pltpu.make_async_copy(
                zero_ref, xout_ref.at[pl.ds(pl.multiple_of(b * bm, bm), bm)], sem)

        def tail_start(b, carry):
            tail_copy(b).start()
            return carry

        def tail_wait(b, carry):
            tail_copy(b).wait()
            return carry

        first_tail = pe_ref[N_EXPERTS - 1] // bm
        n_blk = xout_ref.shape[0] // bm
        lax.fori_loop(first_tail, n_blk, tail_start, 0)
        lax.fori_loop(first_tail, n_blk, tail_wait, 0)

    def body(t, carry):
        for k in range(2):
            e = idx_ref[0, 0, k * tm + t]
            r = idx_ref[0, 0, (2 + k) * tm + t]
            pltpu.make_async_copy(h_ref.at[pl.ds(t, 1)], xout_ref.at[pl.ds(ps_ref[e] + r, 1)],
                                  sem).start()
        return carry

    lax.fori_loop(0, tm, body, 0)
    for _ in range(2):
        pltpu.make_async_copy(h_ref, xout_ref.at[pl.ds(0, tm)], sem).wait()


def _dispatch(pstarts, pends, idx, h3, n_rows):
    t, d = h3.shape
    tm = TM_ROW
    return pl.pallas_call(
        _dispatch_kernel,
        grid_spec=pltpu.PrefetchScalarGridSpec(
            num_scalar_prefetch=2,
            grid=(t // tm,),
            in_specs=[pl.BlockSpec((1, 1, 4 * tm), lambda i, ps, pe: (i, 0, 0),
                                   memory_space=pltpu.SMEM),
                      pl.BlockSpec((tm, d), lambda i, ps, pe: (i, 0))],
            out_specs=pl.BlockSpec(memory_space=pl.ANY),
            scratch_shapes=[pltpu.VMEM((BM_EXPERT, d), F32), pltpu.SemaphoreType.DMA(())]),
        out_shape=jax.ShapeDtypeStruct((n_rows, d), F32),
        compiler_params=pltpu.CompilerParams(dimension_semantics=("arbitrary",)),
        name="moe_dispatch",
    )(pstarts, pends, idx, h3)


def _expert_kernel(be_ref, nu_ref, x_ref, w1_ref, w3_ref, w2_ref, y_ref, w1b, w3b, w2b):
    i = pl.program_id(0)
    prev = be_ref[jnp.maximum(i - 1, 0)]

    @pl.when((i == 0) | (be_ref[i] != prev))
    def _():
        w1b[...] = w1_ref[0].astype(BF16)
        w3b[...] = w3_ref[0].astype(BF16)
        w2b[...] = w2_ref[0].astype(BF16)

    @pl.when(i < nu_ref[0])
    def _():
        xb = x_ref[...].astype(BF16)
        a = _dot(xb, w1b[...])
        g = _dot(xb, w3b[...])
        act = (a / (1.0 + jnp.exp(-a)) * g).astype(BF16)
        y_ref[...] = _dot(act, w2b[...])

    @pl.when(i >= nu_ref[0])
    def _():
        y_ref[...] = jnp.zeros_like(y_ref)


def _experts(blk_e, n_used, x_rows, w1, w3, w2):
    p, d = x_rows.shape
    ff = w1.shape[2]
    bm = BM_EXPERT
    wmap = lambda i, be, nu: (be[i], 0, 0)
    return pl.pallas_call(
        _expert_kernel,
        grid_spec=pltpu.PrefetchScalarGridSpec(
            num_scalar_prefetch=2,
            grid=(p // bm,),
            in_specs=[pl.BlockSpec((bm, d), lambda i, be, nu: (jnp.minimum(i, nu[0] - 1), 0)),
                      pl.BlockSpec((1, d, ff), wmap),
                      pl.BlockSpec((1, d, ff), wmap),
                      pl.BlockSpec((1, ff, d), wmap)],
            out_specs=pl.BlockSpec((bm, d), lambda i, be, nu: (i, 0)),
            scratch_shapes=[pltpu.VMEM((d, ff), BF16), pltpu.VMEM((d, ff), BF16),
                            pltpu.VMEM((ff, d), BF16)]),
        out_shape=jax.ShapeDtypeStruct((p, d), F32),
        compiler_params=pltpu.CompilerParams(
            dimension_semantics=("arbitrary",), vmem_limit_bytes=VMEM_LIMIT),
        name="moe_experts",
    )(blk_e, n_used, x_rows, w1, w3, w2)


def _combine_kernel(ps_ref, idx_ref, x2_ref, route_ref, g_ref, y_ref, o_ref, buf0, buf1, sem):
    tm = x2_ref.shape[0]
    bufs = (buf0, buf1)

    def body(t, carry):
        for k in range(2):
            e = idx_ref[0, 0, k * tm + t]
            r = idx_ref[0, 0, (2 + k) * tm + t]
            pltpu.make_async_copy(y_ref.at[pl.ds(ps_ref[e] + r, 1)], bufs[k].at[pl.ds(t, 1)],
                                  sem).start()
        return carry

    lax.fori_loop(0, tm, body, 0)
    for k in range(2):
        pltpu.make_async_copy(y_ref.at[pl.ds(0, tm)], bufs[k], sem).wait()
    route = route_ref[...]
    x3 = x2_ref[...] + route[:, 2:3] * buf0[...] + route[:, 3:4] * buf1[...]
    o_ref[...] = _rms(x3, g_ref[...])


def _combine(pstarts, idx, x2, route, g, y_rows):
    t, d = x2.shape
    tm = TM_ROW
    row = lambda i, ps: (i, 0)
    return pl.pallas_call(
        _combine_kernel,
        grid_spec=pltpu.PrefetchScalarGridSpec(
            num_scalar_prefetch=1,
            grid=(t // tm,),
            in_specs=[pl.BlockSpec((1, 1, 4 * tm), lambda i, ps: (i, 0, 0), memory_space=pltpu.SMEM),
                      pl.BlockSpec((tm, d), row),
                      pl.BlockSpec((tm, LANES), row),
                      pl.BlockSpec((1, d), lambda i, ps: (0, 0)),
                      pl.BlockSpec(memory_space=pl.ANY)],
            out_specs=pl.BlockSpec((tm, d), row),
            scratch_shapes=[pltpu.VMEM((tm, d), F32), pltpu.VMEM((tm, d), F32),
                            pltpu.SemaphoreType.DMA(())]),
        out_shape=jax.ShapeDtypeStruct((t, d), F32),
        compiler_params=pltpu.CompilerParams(dimension_semantics=("arbitrary",)),
        name="moe_combine",
    )(pstarts, idx, x2, route, g, y_rows)


def kernel(x, mem, positions, mix_norm_g, w_in, conv_dw_w, conv_dw_b, conv_ln_g, conv_ln_b,
           conv_out_g, attn_out_g, w_out, xattn_norm_g, mem_norm_g, w_xq, w_xk, w_xv, w_xo,
           moe_norm_g, w_group, b_group, w_router, b_router, w1, w3, w2, final_norm_g):
    bsz, seq, d = x.shape
    assert w_in.shape[0] == 1, "single-layer encoder only"
    l = 0
    t = bsz * seq
    vec = lambda a: a.reshape(1, -1)
    cos_t, sin_t = _rope_tables(positions)
    xf = x.reshape(t, d)
    c, *qkv = _in_projection(xf, vec(mix_norm_g[l]), w_in[l].astype(BF16), cos_t, sin_t, bsz, seq)
    mixc = _conformer_conv(c.reshape(bsz, seq, CONV_CH), conv_dw_w[l], vec(conv_dw_b[l]),
                           vec(conv_ln_g[l]), vec(conv_ln_b[l]), vec(conv_out_g[l]))
    branches = [_band_attention(*qkv[3 * i:3 * i + 3]) for i in range(len(DILATIONS))]
    kx, vx = _mem_kv(mem, vec(mem_norm_g[l]), w_xk[l].astype(BF16), w_xv[l].astype(BF16))
    pad = LANES - N_GROUPS - N_EXPERTS
    w_rt = jnp.pad(jnp.concatenate([w_group[l], w_router[l]], axis=1), ((0, 0), (0, pad)))
    b_rt = jnp.pad(jnp.concatenate([b_group[l], b_router[l]]), (0, pad)).reshape(1, LANES)
    x2, h3, route, cnt = _post_mix(
        [o for o, _ in branches], [s for _, s in branches], mixc.reshape(t, CONV_CH), xf,
        vec(attn_out_g[l]), w_out[l].astype(BF16), vec(xattn_norm_g[l]), w_xq[l].astype(BF16),
        kx, vx, w_xo[l].astype(BF16), vec(moe_norm_g[l]), w_rt, b_rt, seq)

    bm = BM_EXPERT
    counts = cnt[0, :N_EXPERTS].astype(jnp.int32)
    pcounts = (counts + bm - 1) // bm * bm
    pends = jnp.cumsum(pcounts)
    pstarts = (pends - pcounts).astype(jnp.int32)
    pends = pends.astype(jnp.int32)
    n_blk = (2 * t) // bm + N_EXPERTS
    n_used = pends[-1:] // bm
    blk = jnp.minimum(jnp.arange(n_blk, dtype=jnp.int32), n_used - 1)
    blk_e = jnp.sum((blk[:, None] * bm >= pends[None, :]).astype(jnp.int32), axis=1)
    tr = TM_ROW
    idx = jnp.concatenate([route[:, 0:2], route[:, 4:6]], axis=1).astype(jnp.int32)
    idx = idx.reshape(t // tr, tr, 4).transpose(0, 2, 1).reshape(t // tr, 1, 4 * tr)

    x_rows = _dispatch(pstarts, pends, idx, h3, n_blk * bm)
    y_rows = _experts(blk_e, n_used, x_rows, w1[l], w3[l], w2[l])
    out = _combine(pstarts, idx, x2, route, vec(final_norm_g), y_rows)
    return out.reshape(bsz, seq, d)
```

```python
import functools

import jax
import jax.numpy as jnp
from jax import lax
from jax.experimental import pallas as pl
from jax.experimental.pallas import tpu as pltpu

HEAD_DIM = 64
CONV_CH = 256
ATTN_HEADS = 12
ATTN_WIDTH = ATTN_HEADS * HEAD_DIM
CONV_KERNEL = 31
CONV_PAD = 16
BAND = 64
DILATIONS = (1, 4, 16)
ROPE_THETA = 10000.0
MEM_HEADS = 4
N_GROUPS = 4
EXPERTS_PER_GROUP = 8
N_EXPERTS = N_GROUPS * EXPERTS_PER_GROUP
NORM_EPS = 1e-6
LN_EPS = 1e-5
LANES = 128
NEG = -1e30

TM_IN = 512
TM_POST = 512
POST_HALF = 256
TM_ROW = 512
BM_EXPERT = 256
CONV_CHUNK = 128
ATTN_SUB = 128
VMEM_LIMIT = 56 * 1024 * 1024

F32 = jnp.float32
BF16 = jnp.bfloat16


def _rms(x, g):
    return x * lax.rsqrt(jnp.mean(x * x, axis=-1, keepdims=True) + NORM_EPS) * g


def _dot(a, b):
    return jnp.dot(a, b, preferred_element_type=F32)


def _dot_nt(a, b):
    return lax.dot_general(a, b, (((1,), (1,)), ((), ())), preferred_element_type=F32)


def _rope_kernel(pos_ref, inv_ref, sign_ref, cos_ref, sin_ref):
    ang = pos_ref[...].astype(F32) * inv_ref[...]
    cos_ref[...] = jnp.cos(ang)
    sin_ref[...] = jnp.sin(ang) * sign_ref[...]


def _rope_tables(positions):
    s = positions.shape[0]
    half = HEAD_DIM // 2
    inv = 1.0 / (ROPE_THETA ** (jnp.arange(half, dtype=F32) * (2.0 / HEAD_DIM)))
    inv = jnp.tile(inv, LANES // half)[None, :]
    sign = jnp.tile(jnp.concatenate([-jnp.ones((half,), F32), jnp.ones((half,), F32)]),
                    LANES // HEAD_DIM)[None, :]
    return pl.pallas_call(
        _rope_kernel,
        out_shape=(jax.ShapeDtypeStruct((s, LANES), F32), jax.ShapeDtypeStruct((s, LANES), F32)),
        name="rope_tables",
    )(positions.reshape(s, 1), inv, sign)


def _inproj_kernel(x_ref, g_ref, w_ref, cos_ref, sin_ref, c_ref, *rest):
    out_refs, slab_ref = rest[:-1], rest[-1]
    tm = x_ref.shape[0]
    n_grp = ATTN_WIDTH // LANES
    h = _rms(x_ref[...], g_ref[...]).astype(BF16)
    u = _dot(h, w_ref[:, 0:2 * CONV_CH])
    c_ref[...] = u[:, :CONV_CH] / (1.0 + jnp.exp(-u[:, CONV_CH:]))
    cos = cos_ref[...]
    sin = sin_ref[...]
    lane = lax.broadcasted_iota(jnp.int32, cos.shape, 1)
    first_half = (lane % HEAD_DIM) < (HEAD_DIM // 2)
    off = 2 * CONV_CH
    for which, scale in enumerate((HEAD_DIM ** -0.5, 1.0, None)):
        u = _dot(h, w_ref[:, off:off + ATTN_WIDTH])
        off += ATTN_WIDTH
        for j in range(n_grp):
            xs = u[:, j * LANES:(j + 1) * LANES]
            if scale is not None:
                partner = jnp.where(first_half, pltpu.roll(xs, LANES - 32, 1),
                                    pltpu.roll(xs, 32, 1))
                xs = (xs * cos + partner * sin) * scale
            slab_ref[which * n_grp + j] = xs
        for di, dil in enumerate(DILATIONS):
            ref = out_refs[3 * di + which]
            n = tm // dil
            for r in range(dil):
                for j in range(n_grp):
                    idx = which * n_grp + j
                    rows = slab_ref[idx] if dil == 1 else slab_ref[idx, pl.ds(r, n, stride=dil), :]
                    ref[0, r, :, j * LANES:(j + 1) * LANES] = rows.astype(BF16)


def _in_projection(xf, g, w_in, cos_t, sin_t, bsz, seq):
    t, d = xf.shape
    tm = TM_IN
    n_pos_blk = seq // tm
    row = lambda i: (i, 0)
    const = lambda i: (0, 0)
    cls = lambda i: (i // n_pos_blk, 0, i % n_pos_blk, 0)
    out_specs = [pl.BlockSpec((tm, CONV_CH), row)]
    out_shape = [jax.ShapeDtypeStruct((t, CONV_CH), F32)]
    for dil in DILATIONS:
        for _ in range(3):
            out_specs.append(pl.BlockSpec((1, dil, tm // dil, ATTN_WIDTH), cls))
            out_shape.append(jax.ShapeDtypeStruct((bsz, dil, seq // dil, ATTN_WIDTH), BF16))
    return pl.pallas_call(
        _inproj_kernel,
        grid=(t // tm,),
        in_specs=[
            pl.BlockSpec((tm, d), row),
            pl.BlockSpec((1, d), const),
            pl.BlockSpec(w_in.shape, const),
            pl.BlockSpec((tm, LANES), lambda i: (i % n_pos_blk, 0)),
            pl.BlockSpec((tm, LANES), lambda i: (i % n_pos_blk, 0)),
        ],
        out_specs=out_specs,
        out_shape=out_shape,
        scratch_shapes=[pltpu.VMEM((3 * ATTN_WIDTH // LANES, tm, LANES), F32)],
        compiler_params=pltpu.CompilerParams(
            dimension_semantics=("parallel",), vmem_limit_bytes=VMEM_LIMIT),
        name="in_projection",
    )(xf, g, w_in, cos_t, sin_t)


def _conv_kernel(c_ref, w_ref, b_ref, lng_ref, lnb_ref, og_ref, o_ref, pad_ref):
    seq = c_ref.shape[1]
    zeros = jnp.zeros((CONV_PAD, CONV_CH), F32)
    pad_ref[0:CONV_PAD, :] = zeros
    pad_ref[seq + CONV_PAD:seq + 2 * CONV_PAD, :] = zeros
    pad_ref[CONV_PAD:seq + CONV_PAD, :] = c_ref[0]
    shift = CONV_PAD - CONV_KERNEL // 2

    def body(i, carry):
        base = pl.multiple_of(i * CONV_CHUNK, CONV_CHUNK)
        win = pad_ref[pl.ds(base, CONV_CHUNK + 2 * CONV_PAD), :]
        acc = jnp.zeros((CONV_CHUNK, CONV_CH), F32)
        for b in range(8):
            taps = [k for k in range(CONV_KERNEL) if k % 8 == b]
            span = CONV_CHUNK + 8 * (len(taps) - 1)
            sb = win[b + shift:b + shift + span, :]
            for a, k in enumerate(taps):
                acc = acc + sb[8 * a:8 * a + CONV_CHUNK, :] * w_ref[k:k + 1, :]
        acc = acc + b_ref[...]
        mu = jnp.mean(acc, axis=-1, keepdims=True)
        cen = acc - mu
        var = jnp.mean(cen * cen, axis=-1, keepdims=True)
        y = cen * lax.rsqrt(var + LN_EPS) * lng_ref[...] + lnb_ref[...]
        y = y / (1.0 + jnp.exp(-y))
        o_ref[0, pl.ds(base, CONV_CHUNK), :] = _rms(y, og_ref[...]).astype(BF16)
        return carry

    lax.fori_loop(0, seq // CONV_CHUNK, body, 0)


def _conformer_conv(c, w, b, lng, lnb, og):
    bsz, seq, ch = c.shape
    vec = pl.BlockSpec((1, ch), lambda i: (0, 0))
    return pl.pallas_call(
        _conv_kernel,
        grid=(bsz,),
        in_specs=[pl.BlockSpec((1, seq, ch), lambda i: (i, 0, 0)),
                  pl.BlockSpec(w.shape, lambda i: (0, 0)), vec, vec, vec, vec],
        out_specs=pl.BlockSpec((1, seq, ch), lambda i: (i, 0, 0)),
        out_shape=jax.ShapeDtypeStruct((bsz, seq, ch), BF16),
        scratch_shapes=[pltpu.VMEM((seq + 2 * CONV_PAD, ch), F32)],
        compiler_params=pltpu.CompilerParams(
            dimension_semantics=("parallel",), vmem_limit_bytes=VMEM_LIMIT),
        name="conformer_conv",
    )(c, w, b, lng, lnb, og)


def _band_attn_kernel(q_ref, k_ref, v_ref, o_ref, lse_ref, *, sub_len, q_blk, win):
    sq = ATTN_SUB
    row = lax.broadcasted_iota(jnp.int32, (2 * sq, win), 0) % sq
    col = lax.broadcasted_iota(jnp.int32, (2 * sq, win), 1)
    lane = lax.broadcasted_iota(jnp.int32, (sq, LANES), 1)
    is_a = lane < HEAD_DIM
    for sub in range(q_blk // sq):
        q0 = pl.program_id(2) * q_blk + sub * sq
        ws = pl.multiple_of(jnp.clip(q0 - BAND, 0, sub_len - win), BAND)
        bias = jnp.where(jnp.abs(row - col + (q0 - ws)) <= BAND, 0.0, NEG).astype(F32)
        rows = slice(sub * sq, (sub + 1) * sq)
        for hp in range(ATTN_WIDTH // LANES):
            sl = slice(hp * LANES, (hp + 1) * LANES)
            qh = q_ref[rows, sl]
            kh = k_ref[pl.ds(ws, win), sl]
            vh = v_ref[pl.ds(ws, win), sl]
            zero = jnp.zeros_like(qh)
            q2 = jnp.concatenate([jnp.where(is_a, qh, zero), jnp.where(is_a, zero, qh)], axis=0)
            s = _dot_nt(q2, kh) + bias
            m = jnp.max(s, axis=1, keepdims=True)
            p = jnp.exp(s - m)
            l = jnp.sum(p, axis=1, keepdims=True)
            o2 = _dot(p.astype(BF16), vh) / l
            lse2 = m + jnp.log(l)
            o_ref[rows, sl] = jnp.where(is_a, o2[:sq], o2[sq:]).astype(BF16)
            lse_ref[rows, sl] = jnp.where(is_a, lse2[:sq], lse2[sq:])


def _band_attention(q, k, v):
    bsz, dil, sub_len, width = q.shape
    q_blk = min(256, sub_len)
    win = min(sub_len, ATTN_SUB + 2 * BAND)
    qmap = lambda b, r, i: (b, r, i, 0)
    kmap = lambda b, r, i: (b, r, 0, 0)
    return pl.pallas_call(
        functools.partial(_band_attn_kernel, sub_len=sub_len, q_blk=q_blk, win=win),
        grid=(bsz, dil, sub_len // q_blk),
        in_specs=[pl.BlockSpec((None, None, q_blk, width), qmap),
                  pl.BlockSpec((None, None, sub_len, width), kmap),
                  pl.BlockSpec((None, None, sub_len, width), kmap)],
        out_specs=[pl.BlockSpec((None, None, q_blk, width), qmap),
                   pl.BlockSpec((None, None, q_blk, width), qmap)],
        out_shape=[jax.ShapeDtypeStruct((bsz, dil, sub_len, width), BF16),
                   jax.ShapeDtypeStruct((bsz, dil, sub_len, width), F32)],
        compiler_params=pltpu.CompilerParams(
            dimension_semantics=("parallel", "parallel", "arbitrary"),
            vmem_limit_bytes=VMEM_LIMIT),
        name=f"band_attention_d{dil}",
    )(q, k, v)


def _memkv_kernel(m_ref, g_ref, wk_ref, wv_ref, k_ref, v_ref):
    h = _rms(m_ref[0], g_ref[...]).astype(BF16)
    k_ref[0] = _dot(h, wk_ref[...]).astype(BF16)
    v_ref[0] = _dot(h, wv_ref[...]).astype(BF16)


def _mem_kv(mem, g, wk, wv):
    bsz, m, d = mem.shape
    blk = pl.BlockSpec((1, m, d), lambda i: (i, 0, 0))
    const = lambda i: (0, 0)
    return pl.pallas_call(
        _memkv_kernel,
        grid=(bsz,),
        in_specs=[blk, pl.BlockSpec((1, d), const), pl.BlockSpec((d, d), const),
                  pl.BlockSpec((d, d), const)],
        out_specs=[blk, blk],
        out_shape=[jax.ShapeDtypeStruct((bsz, m, d), BF16)] * 2,
        compiler_params=pltpu.CompilerParams(dimension_semantics=("parallel",)),
        name="mem_kv",
    )(mem, g, wk, wv)


def _post_kernel(o1_ref, o2_ref, o3_ref, l1_ref, l2_ref, l3_ref, mc_ref, x_ref,
                 ag_ref, wout_ref, xg_ref, wq_ref, kx_ref, vx_ref, wo_ref,
                 mg_ref, wr_ref, br_ref,
                 x2_ref, h3_ref, route_ref, tot_ref,
                 slab_ref, *scratch):
    tm = x_ref.shape[0]
    d = x_ref.shape[1]
    n_grp = ATTN_WIDTH // LANES

    for bi, (o_ref, l_ref) in enumerate(((o2_ref, l2_ref), (o3_ref, l3_ref))):
        dil = o_ref.shape[1]
        n = tm // dil
        base = bi * 2 * n_grp
        for r in range(dil):
            rows = pl.ds(r, n, stride=dil)
            for j in range(n_grp):
                sl = slice(j * LANES, (j + 1) * LANES)
                slab_ref[base + j, rows, :] = o_ref[0, r, :, sl].astype(F32)
                slab_ref[base + n_grp + j, rows, :] = l_ref[0, r, :, sl]

    hm = POST_HALF
    lane = lax.broadcasted_iota(jnp.int32, (hm, LANES), 1)
    is_a = lane < HEAD_DIM
    w_rt = wr_ref[...]
    w_hi = w_rt.astype(BF16)
    w_lo = (w_rt - w_hi.astype(F32)).astype(BF16)

    def first_max(vals):
        vmax = jnp.max(vals, axis=1, keepdims=True)
        idx = jnp.min(jnp.where(vals == vmax, lane, LANES), axis=1, keepdims=True)
        return vmax, idx

    tot_box = [jnp.zeros((1, LANES), F32)]

    def chain(r0, mixed_ref, ox_ref):
        rs = slice(r0, r0 + hm)

        pieces = []
        ssq = jnp.zeros((hm, 1), F32)
        for hp in range(n_grp):
            sl = slice(hp * LANES, (hp + 1) * LANES)
            e1 = l1_ref[0, 0, rs, sl]
            e2 = slab_ref[n_grp + hp, rs, :]
            e3 = slab_ref[3 * n_grp + hp, rs, :]
            mx = jnp.maximum(jnp.maximum(e1, e2), e3)
            w1, w2, w3 = jnp.exp(e1 - mx), jnp.exp(e2 - mx), jnp.exp(e3 - mx)
            num = (w1 * o1_ref[0, 0, rs, sl].astype(F32) + w2 * slab_ref[hp, rs, :]
                   + w3 * slab_ref[2 * n_grp + hp, rs, :])
            a = num / (w1 + w2 + w3)
            pieces.append(a)
            ssq = ssq + jnp.sum(a * a, axis=1, keepdims=True)
        inv = lax.rsqrt(ssq * (1.0 / ATTN_WIDTH) + NORM_EPS)
        mixed_ref[:, 0:CONV_CH] = mc_ref[rs, :]
        for hp, a in enumerate(pieces):
            sl = slice(hp * LANES, (hp + 1) * LANES)
            mixed_ref[:, CONV_CH + hp * LANES:CONV_CH + (hp + 1) * LANES] = (
                a * inv * ag_ref[:, sl]).astype(BF16)
        yield
        x1 = x_ref[rs, :] + _dot(mixed_ref[...], wout_ref[...])
        yield

        hd = d // MEM_HEADS
        qx = (_dot(_rms(x1, xg_ref[...]).astype(BF16), wq_ref[...]) * (hd ** -0.5)).astype(BF16)
        yield
        for h in range(MEM_HEADS):
            sl = slice(h * hd, (h + 1) * hd)
            s = _dot_nt(qx[:, sl], kx_ref[0, :, sl])
            m = jnp.max(s, axis=1, keepdims=True)
            p = jnp.exp(s - m)
            l = jnp.sum(p, axis=1, keepdims=True)
            ox_ref[:, sl] = (_dot(p.astype(BF16), vx_ref[0, :, sl]) / l).astype(BF16)
        yield
        x2 = x1 + _dot(ox_ref[...], wo_ref[...])
        x2_ref[rs, :] = x2
        yield

        h3 = _rms(x2, mg_ref[...])
        h3_ref[rs, :] = h3.astype(BF16)
        h_hi = h3.astype(BF16)
        h_lo = (h3 - h_hi.astype(F32)).astype(BF16)
        lg = _dot(h_hi, w_hi) + (_dot(h_hi, w_lo) + _dot(h_lo, w_hi)) + br_ref[...]

        is_group = lane < N_GROUPS
        gmax, gidx = first_max(jnp.where(is_group, lg, NEG))
        p_g = 1.0 / jnp.sum(jnp.where(is_group, jnp.exp(lg - gmax), 0.0), axis=1, keepdims=True)
        lo = N_GROUPS + EXPERTS_PER_GROUP * gidx
        el = jnp.where((lane >= lo) & (lane < lo + EXPERTS_PER_GROUP), lg, NEG)
        v1, i1 = first_max(el)
        v2, i2 = first_max(jnp.where(lane == i1, NEG, el))
        t2 = jnp.exp(v2 - v1)
        g1 = p_g / (1.0 + t2)
        g2 = g1 * t2
        e1 = i1 - N_GROUPS
        e2 = i2 - N_GROUPS

        hits = jnp.where((lane == e1) | (lane == e2), 1.0, 0.0)
        tot_box[0] = tot_box[0] + jnp.sum(hits, axis=0, keepdims=True)

        route = jnp.zeros((hm, LANES), F32)
        for pos, val in enumerate((e1.astype(F32), e2.astype(F32), g1, g2)):
            route = jnp.where(lane == pos, val, route)
        route_ref[rs, :] = route
        yield

    chains = [chain(i * hm, scratch[2 * i], scratch[2 * i + 1]) for i in range(tm // hm)]
    n_stage = 6
    for step in range(n_stage + len(chains) - 1):
        for ci, ch in enumerate(chains):
            if 0 <= step - ci < n_stage:
                next(ch)
    tot_ref[0] = tot_box[0]


def _post_mix(o_list, lse_list, mixc, xf, ag, w_out, xg, w_xq, kx, vx, w_xo, mg, w_rt, b_rt, seq):
    t, d = xf.shape
    tm = TM_POST
    blk_per_seq = seq // tm
    row = lambda i: (i, 0)
    const = lambda i: (0, 0)
    bmap = lambda i: (i // blk_per_seq, 0, 0)
    mlen = kx.shape[1]
    cls = lambda i: (i // blk_per_seq, 0, i % blk_per_seq, 0)
    wide = [pl.BlockSpec((1, dil, tm // dil, ATTN_WIDTH), cls) for dil in DILATIONS]
    narrow = pl.BlockSpec((tm, LANES), row)
    full = pl.BlockSpec((tm, d), row)
    mat = pl.BlockSpec((d, d), const)
    n_slab = 4 * (ATTN_WIDTH // LANES)
    return pl.pallas_call(
        _post_kernel,
        grid=(t // tm,),
        in_specs=[*wide, *wide,
                  pl.BlockSpec((tm, CONV_CH), row), full,
                  pl.BlockSpec((1, ATTN_WIDTH), const), mat,
                  pl.BlockSpec((1, d), const), mat,
                  pl.BlockSpec((1, mlen, d), bmap), pl.BlockSpec((1, mlen, d), bmap), mat,
                  pl.BlockSpec((1, d), const), pl.BlockSpec((d, LANES), const),
                  pl.BlockSpec((1, LANES), const)],
        out_specs=[full, full, narrow, pl.BlockSpec((1, 1, LANES), lambda i: (i, 0, 0))],
        out_shape=[jax.ShapeDtypeStruct((t, d), F32), jax.ShapeDtypeStruct((t, d), BF16),
                   jax.ShapeDtypeStruct((t, LANES), F32),
                   jax.ShapeDtypeStruct((t // tm, 1, LANES), F32)],
        scratch_shapes=[pltpu.VMEM((n_slab, tm, LANES), F32)]
        + [pltpu.VMEM((POST_HALF, d), BF16)] * (2 * (tm // POST_HALF)),
        compiler_params=pltpu.CompilerParams(
            dimension_semantics=("parallel",), vmem_limit_bytes=VMEM_LIMIT),
        name="post_mix",
    )(*o_list, *lse_list, mixc, xf, ag, w_out, xg, w_xq, kx, vx, w_xo, mg, w_rt, b_rt)


ROW_ALIGN = 8
RUN_PIECES = tuple(2 ** k for k in range(9, 2, -1))
TILE_PIECES = (1024,) + RUN_PIECES
LOCAL_ROWS = 2 * TM_ROW + ROW_ALIGN * N_EXPERTS


def _run_copies(tab_ref, local_ref, global_ref, sem, to_global):
    def body(e, carry):
        ls = tab_ref[0, 0, e]
        gs = tab_ref[0, 0, N_EXPERTS + e]
        n = tab_ref[0, 0, 2 * N_EXPERTS + e]
        for size in RUN_PIECES:
            done = (n // (2 * size)) * (2 * size)

            @pl.when((n & size) != 0)
            def _():
                loc = local_ref.at[pl.ds(pl.multiple_of(ls + done, ROW_ALIGN), size)]
                glo = global_ref.at[pl.ds(pl.multiple_of(gs + done, ROW_ALIGN), size)]
                src, dst = (loc, glo) if to_global else (glo, loc)
                pltpu.make_async_copy(src, dst, sem).start()
        return carry

    lax.fori_loop(0, N_EXPERTS, body, 0)


def _run_waits(total, local_ref, global_ref, sem):
    for size in TILE_PIECES:
        @pl.when((total & size) != 0)
        def _():
            pltpu.make_async_copy(local_ref.at[pl.ds(0, size)], global_ref.at[pl.ds(0, size)],
                                  sem).wait()


def _local_positions(route, lstart):
    tm = route.shape[0]
    lane = lax.broadcasted_iota(jnp.int32, (tm, LANES), 1)
    rr = lax.broadcasted_iota(jnp.int32, (tm, tm), 0)
    cc = lax.broadcasted_iota(jnp.int32, (tm, tm), 1)
    ltri = jnp.where(cc < rr, 1.0, 0.0).astype(BF16)
    oh1 = jnp.where(lane == route[:, 0:1].astype(jnp.int32), 1.0, 0.0)
    oh2 = jnp.where(lane == route[:, 1:2].astype(jnp.int32), 1.0, 0.0)
    pre1 = _dot(ltri, oh1.astype(BF16))
    pre2 = _dot(ltri, oh2.astype(BF16))
    tot1 = jnp.sum(oh1, axis=0, keepdims=True)
    lp1 = jnp.sum(oh1 * (lstart + pre1), axis=1, keepdims=True)
    lp2 = jnp.sum(oh2 * (lstart + tot1 + pre2), axis=1, keepdims=True)
    return lp1, lp2


def _dispatch_kernel(ps_ref, pe_ref, tab_ref, ls_ref, route_ref, h_ref, xout_ref, lp_ref,
                     zero_ref, xs_ref, pend_ref, sems):
    tm = h_ref.shape[0]
    bm = zero_ref.shape[0]
    sem = sems.at[2]

    @pl.when(pl.program_id(0) == 0)
    def _():
        zero_ref[...] = jnp.zeros_like(zero_ref)

        def pad_copy(e):
            last = pl.multiple_of(pe_ref[e] - bm, bm)
            return pltpu.make_async_copy(zero_ref, xout_ref.at[pl.ds(last, bm)], sem)

        def nonempty(e):
            return pe_ref[e] > ps_ref[e]

        def start(e, carry):
            pl.when(nonempty(e))(lambda: pad_copy(e).start())
            return carry

        def wait(e, carry):
            pl.when(nonempty(e))(lambda: pad_copy(e).wait())
            return carry

        lax.fori_loop(0, N_EXPERTS, start, 0)
        lax.fori_loop(0, N_EXPERTS, wait, 0)

        def tail_copy(b):
            return pltpu.make_async_copy(
                zero_ref, xout_ref.at[pl.ds(pl.multiple_of(b * bm, bm), bm)], sem)

        def tail_start(b, carry):
            tail_copy(b).start()
            return carry

        def tail_wait(b, carry):
            tail_copy(b).wait()
            return carry

        first_tail = pe_ref[N_EXPERTS - 1] // bm
        n_blk = xout_ref.shape[0] // bm
        lax.fori_loop(first_tail, n_blk, tail_start, 0)
        lax.fori_loop(first_tail, n_blk, tail_wait, 0)

    i = pl.program_id(0)
    n_step = pl.num_programs(0)
    slot = i % 2

    def drain(s):
        _run_waits(pend_ref[s], xs_ref.at[s], xout_ref, sems.at[s])

    @pl.when(i >= 2)
    def _():
        drain(slot)

    lp1, lp2 = _local_positions(route_ref[...], ls_ref[0])
    lane = lax.broadcasted_iota(jnp.int32, (tm, LANES), 1)
    lp_ref[...] = jnp.where(lane == 0, lp1, jnp.where(lane == 1, lp2, 0.0))
    row1 = jnp.broadcast_to(lp1, (tm, LANES)).T[0:1, :].astype(jnp.int32)
    row2 = jnp.broadcast_to(lp2, (tm, LANES)).T[0:1, :].astype(jnp.int32)
    slot_id = lax.broadcasted_iota(jnp.int32, (LOCAL_ROWS, tm), 0)
    perm = jnp.where((slot_id == row1) | (slot_id == row2), 1.0, 0.0).astype(BF16)
    xs_ref[slot] = _dot(perm, h_ref[...])
    _run_copies(tab_ref, xs_ref.at[slot], xout_ref, sems.at[slot], to_global=True)
    pend_ref[slot] = tab_ref[0, 0, 3 * N_EXPERTS]

    @pl.when(i == n_step - 1)
    def _():
        drain(slot)

        @pl.when(n_step > 1)
        def _():
            drain(1 - slot)


def _dispatch(pstarts, pends, tab, lstart, route, h3, n_rows):
    t, d = h3.shape
    tm = TM_ROW
    tile = lambda i, ps, pe: (i, 0, 0)
    row = lambda i, ps, pe: (i, 0)
    return pl.pallas_call(
        _dispatch_kernel,
        grid_spec=pltpu.PrefetchScalarGridSpec(
            num_scalar_prefetch=2,
            grid=(t // tm,),
            in_specs=[pl.BlockSpec((1, 1, LANES), tile, memory_space=pltpu.SMEM),
                      pl.BlockSpec((1, 1, LANES), tile),
                      pl.BlockSpec((tm, LANES), row),
                      pl.BlockSpec((tm, d), row)],
            out_specs=[pl.BlockSpec(memory_space=pl.ANY), pl.BlockSpec((tm, LANES), row)],
            scratch_shapes=[pltpu.VMEM((BM_EXPERT, d), F32), pltpu.VMEM((2, LOCAL_ROWS, d), F32),
                            pltpu.SMEM((2,), jnp.int32), pltpu.SemaphoreType.DMA((3,))]),
        out_shape=[jax.ShapeDtypeStruct((n_rows, d), F32),
                   jax.ShapeDtypeStruct((t, LANES), F32)],
        compiler_params=pltpu.CompilerParams(
            dimension_semantics=("arbitrary",), vmem_limit_bytes=VMEM_LIMIT),
        name="moe_dispatch",
    )(pstarts, pends, tab, lstart, route, h3)


def _expert_kernel(be_ref, nu_ref, x_ref, w1_ref, w3_ref, w2_ref, y_ref, w1b, w3b, w2b):
    i = pl.program_id(0)
    prev = be_ref[jnp.maximum(i - 1, 0)]

    @pl.when((i == 0) | (be_ref[i] != prev))
    def _():
        w1b[...] = w1_ref[0].astype(BF16)
        w3b[...] = w3_ref[0].astype(BF16)
        w2b[...] = w2_ref[0].astype(BF16)

    @pl.when(i < nu_ref[0])
    def _():
        xb = x_ref[...].astype(BF16)
        a = _dot(xb, w1b[...])
        g = _dot(xb, w3b[...])
        act = (a / (1.0 + jnp.exp(-a)) * g).astype(BF16)
        y_ref[...] = _dot(act, w2b[...])

    @pl.when(i >= nu_ref[0])
    def _():
        y_ref[...] = jnp.zeros_like(y_ref)


def _experts(blk_e, n_used, x_rows, w1, w3, w2):
    p, d = x_rows.shape
    ff = w1.shape[2]
    bm = BM_EXPERT
    wmap = lambda i, be, nu: (be[i], 0, 0)
    return pl.pallas_call(
        _expert_kernel,
        grid_spec=pltpu.PrefetchScalarGridSpec(
            num_scalar_prefetch=2,
            grid=(p // bm,),
            in_specs=[pl.BlockSpec((bm, d), lambda i, be, nu: (jnp.minimum(i, nu[0] - 1), 0)),
                      pl.BlockSpec((1, d, ff), wmap),
                      pl.BlockSpec((1, d, ff), wmap),
                      pl.BlockSpec((1, ff, d), wmap)],
            out_specs=pl.BlockSpec((bm, d), lambda i, be, nu: (i, 0)),
            scratch_shapes=[pltpu.VMEM((d, ff), BF16), pltpu.VMEM((d, ff), BF16),
                            pltpu.VMEM((ff, d), BF16)]),
        out_shape=jax.ShapeDtypeStruct((p, d), F32),
        compiler_params=pltpu.CompilerParams(
            dimension_semantics=("arbitrary",), vmem_limit_bytes=VMEM_LIMIT),
        name="moe_experts",
    )(blk_e, n_used, x_rows, w1, w3, w2)


def _combine_kernel(tab_ref, nxt_ref, x2_ref, route_ref, lp_ref, g_ref, y_ref, o_ref, ys_ref, sems):
    tm = x2_ref.shape[0]
    i = pl.program_id(0)
    n_step = pl.num_programs(0)
    slot = i % 2

    @pl.when(i == 0)
    def _():
        ys_ref[...] = jnp.zeros_like(ys_ref)
        _run_copies(tab_ref, ys_ref.at[0], y_ref, sems.at[0], to_global=False)

    @pl.when(i + 1 < n_step)
    def _():
        _run_copies(nxt_ref, ys_ref.at[1 - slot], y_ref, sems.at[1 - slot], to_global=False)

    _run_waits(tab_ref[0, 0, 3 * N_EXPERTS], ys_ref.at[slot], y_ref, sems.at[slot])

    route = route_ref[...]
    lp = lp_ref[...]
    slot_id = lax.broadcasted_iota(jnp.int32, (tm, LOCAL_ROWS), 1)
    sel = (jnp.where(slot_id == lp[:, 0:1].astype(jnp.int32), route[:, 2:3], 0.0)
           + jnp.where(slot_id == lp[:, 1:2].astype(jnp.int32), route[:, 3:4], 0.0))
    x3 = x2_ref[...] + _dot(sel.astype(BF16), ys_ref[slot].astype(BF16))
    o_ref[...] = _rms(x3, g_ref[...])


def _combine(tab, x2, route, lp, g, y_rows):
    t, d = x2.shape
    tm = TM_ROW
    n_tile = t // tm
    row = lambda i: (i, 0)
    return pl.pallas_call(
        _combine_kernel,
        grid=(n_tile,),
        in_specs=[pl.BlockSpec((1, 1, LANES), lambda i: (i, 0, 0), memory_space=pltpu.SMEM),
                  pl.BlockSpec((1, 1, LANES), lambda i: (jnp.minimum(i + 1, n_tile - 1), 0, 0),
                               memory_space=pltpu.SMEM),
                  pl.BlockSpec((tm, d), row),
                  pl.BlockSpec((tm, LANES), row),
                  pl.BlockSpec((tm, LANES), row),
                  pl.BlockSpec((1, d), lambda i: (0, 0)),
                  pl.BlockSpec(memory_space=pl.ANY)],
        out_specs=pl.BlockSpec((tm, d), row),
        out_shape=jax.ShapeDtypeStruct((t, d), F32),
        scratch_shapes=[pltpu.VMEM((2, LOCAL_ROWS, d), F32), pltpu.SemaphoreType.DMA((2,))],
        compiler_params=pltpu.CompilerParams(
            dimension_semantics=("arbitrary",), vmem_limit_bytes=VMEM_LIMIT),
        name="moe_combine",
    )(tab, tab, x2, route, lp, g, y_rows)


def kernel(x, mem, positions, mix_norm_g, w_in, conv_dw_w, conv_dw_b, conv_ln_g, conv_ln_b,
           conv_out_g, attn_out_g, w_out, xattn_norm_g, mem_norm_g, w_xq, w_xk, w_xv, w_xo,
           moe_norm_g, w_group, b_group, w_router, b_router, w1, w3, w2, final_norm_g):
    bsz, seq, d = x.shape
    assert w_in.shape[0] == 1, "single-layer encoder only"
    l = 0
    t = bsz * seq
    vec = lambda a: a.reshape(1, -1)
    cos_t, sin_t = _rope_tables(positions)
    xf = x.reshape(t, d)
    c, *qkv = _in_projection(xf, vec(mix_norm_g[l]), w_in[l].astype(BF16), cos_t, sin_t, bsz, seq)
    mixc = _conformer_conv(c.reshape(bsz, seq, CONV_CH), conv_dw_w[l], vec(conv_dw_b[l]),
                           vec(conv_ln_g[l]), vec(conv_ln_b[l]), vec(conv_out_g[l]))
    branches = [_band_attention(*qkv[3 * i:3 * i + 3]) for i in range(len(DILATIONS))]
    kx, vx = _mem_kv(mem, vec(mem_norm_g[l]), w_xk[l].astype(BF16), w_xv[l].astype(BF16))
    pad = LANES - N_GROUPS - N_EXPERTS
    w_rt = jnp.pad(jnp.concatenate([w_group[l], w_router[l]], axis=1), ((0, 0), (0, pad)))
    b_rt = jnp.pad(jnp.concatenate([b_group[l], b_router[l]]), (0, pad)).reshape(1, LANES)
    x2, h3, route, tile_tot = _post_mix(
        [o for o, _ in branches], [s for _, s in branches], mixc.reshape(t, CONV_CH), xf,
        vec(attn_out_g[l]), w_out[l].astype(BF16), vec(xattn_norm_g[l]), w_xq[l].astype(BF16),
        kx, vx, w_xo[l].astype(BF16), vec(moe_norm_g[l]), w_rt, b_rt, seq)

    assert TM_ROW == TM_POST
    bm = BM_EXPERT
    n_tile = t // TM_ROW
    tt = tile_tot[:, 0, :N_EXPERTS].astype(jnp.int32)
    tt = (tt + ROW_ALIGN - 1) // ROW_ALIGN * ROW_ALIGN
    lstart = jnp.cumsum(tt, axis=1) - tt
    before = jnp.cumsum(tt, axis=0) - tt
    pcounts = (jnp.sum(tt, axis=0) + bm - 1) // bm * bm
    pends = jnp.cumsum(pcounts).astype(jnp.int32)
    pstarts = pends - pcounts
    n_blk = (2 * t + (ROW_ALIGN - 1) * n_tile * N_EXPERTS) // bm + N_EXPERTS
    n_used = pends[-1:] // bm
    blk = jnp.minimum(jnp.arange(n_blk, dtype=jnp.int32), n_used - 1)
    blk_e = jnp.sum((blk[:, None] * bm >= pends[None, :]).astype(jnp.int32), axis=1)
    tile_rows = jnp.sum(tt, axis=1, keepdims=True)
    tab = jnp.concatenate([lstart, pstarts[None, :] + before, tt, tile_rows,
                           jnp.zeros((n_tile, N_EXPERTS - 1), jnp.int32)], axis=1)[:, None, :]
    lstart_v = jnp.pad(lstart.astype(F32), ((0, 0), (0, LANES - N_EXPERTS)))[:, None, :]

    x_rows, lp = _dispatch(pstarts, pends, tab, lstart_v, route, h3, n_blk * bm)
    y_rows = _experts(blk_e, n_used, x_rows, w1[l], w3[l], w2[l])
    out = _combine(tab, x2, route, lp, vec(final_norm_g), y_rows)
    return out.reshape(bsz, seq, d)
```

```python
import functools

import jax
import jax.numpy as jnp
from jax import lax
from jax.experimental import pallas as pl
from jax.experimental.pallas import tpu as pltpu

HEAD_DIM = 64
CONV_CH = 256
ATTN_HEADS = 12
ATTN_WIDTH = ATTN_HEADS * HEAD_DIM
CONV_KERNEL = 31
CONV_PAD = 16
BAND = 64
DILATIONS = (1, 4, 16)
ROPE_THETA = 10000.0
MEM_HEADS = 4
N_GROUPS = 4
EXPERTS_PER_GROUP = 8
N_EXPERTS = N_GROUPS * EXPERTS_PER_GROUP
NORM_EPS = 1e-6
LN_EPS = 1e-5
LANES = 128
NEG = -1e30

TM_IN = 512
TM_POST = 512
POST_HALF = 256
TM_ROW = 512
BM_EXPERT = 512
CONV_CHUNK = 128
ATTN_SUB = 128
VMEM_LIMIT = 56 * 1024 * 1024

F32 = jnp.float32
BF16 = jnp.bfloat16


def _rms(x, g):
    return x * lax.rsqrt(jnp.mean(x * x, axis=-1, keepdims=True) + NORM_EPS) * g


def _dot(a, b):
    return jnp.dot(a, b, preferred_element_type=F32)


def _dot_nt(a, b):
    return lax.dot_general(a, b, (((1,), (1,)), ((), ())), preferred_element_type=F32)


def _rope_kernel(pos_ref, inv_ref, sign_ref, cos_ref, sin_ref):
    ang = pos_ref[...].astype(F32) * inv_ref[...]
    cos_ref[...] = jnp.cos(ang)
    sin_ref[...] = jnp.sin(ang) * sign_ref[...]


def _rope_tables(positions):
    s = positions.shape[0]
    half = HEAD_DIM // 2
    inv = 1.0 / (ROPE_THETA ** (jnp.arange(half, dtype=F32) * (2.0 / HEAD_DIM)))
    inv = jnp.tile(inv, LANES // half)[None, :]
    sign = jnp.tile(jnp.concatenate([-jnp.ones((half,), F32), jnp.ones((half,), F32)]),
                    LANES // HEAD_DIM)[None, :]
    return pl.pallas_call(
        _rope_kernel,
        out_shape=(jax.ShapeDtypeStruct((s, LANES), F32), jax.ShapeDtypeStruct((s, LANES), F32)),
        name="rope_tables",
    )(positions.reshape(s, 1), inv, sign)


def _inproj_kernel(x_ref, g_ref, w_ref, cos_ref, sin_ref, c_ref, *rest):
    out_refs, slab_ref = rest[:-1], rest[-1]
    tm = x_ref.shape[0]
    n_grp = ATTN_WIDTH // LANES
    h = _rms(x_ref[...], g_ref[...]).astype(BF16)
    u = _dot(h, w_ref[:, 0:2 * CONV_CH])
    c_ref[...] = u[:, :CONV_CH] / (1.0 + jnp.exp(-u[:, CONV_CH:]))
    cos = cos_ref[...]
    sin = sin_ref[...]
    lane = lax.broadcasted_iota(jnp.int32, cos.shape, 1)
    first_half = (lane % HEAD_DIM) < (HEAD_DIM // 2)
    off = 2 * CONV_CH
    for which, scale in enumerate((HEAD_DIM ** -0.5, 1.0, None)):
        u = _dot(h, w_ref[:, off:off + ATTN_WIDTH])
        off += ATTN_WIDTH
        for j in range(n_grp):
            xs = u[:, j * LANES:(j + 1) * LANES]
            if scale is not None:
                partner = jnp.where(first_half, pltpu.roll(xs, LANES - 32, 1),
                                    pltpu.roll(xs, 32, 1))
                xs = (xs * cos + partner * sin) * scale
            slab_ref[which * n_grp + j] = xs
        for di, dil in enumerate(DILATIONS):
            ref = out_refs[3 * di + which]
            n = tm // dil
            for r in range(dil):
                for j in range(n_grp):
                    idx = which * n_grp + j
                    rows = slab_ref[idx] if dil == 1 else slab_ref[idx, pl.ds(r, n, stride=dil), :]
                    ref[0, r, :, j * LANES:(j + 1) * LANES] = rows.astype(BF16)


def _in_projection(xf, g, w_in, cos_t, sin_t, bsz, seq):
    t, d = xf.shape
    tm = TM_IN
    n_pos_blk = seq // tm
    row = lambda i: (i, 0)
    const = lambda i: (0, 0)
    cls = lambda i: (i // n_pos_blk, 0, i % n_pos_blk, 0)
    out_specs = [pl.BlockSpec((tm, CONV_CH), row)]
    out_shape = [jax.ShapeDtypeStruct((t, CONV_CH), F32)]
    for dil in DILATIONS:
        for _ in range(3):
            out_specs.append(pl.BlockSpec((1, dil, tm // dil, ATTN_WIDTH), cls))
            out_shape.append(jax.ShapeDtypeStruct((bsz, dil, seq // dil, ATTN_WIDTH), BF16))
    return pl.pallas_call(
        _inproj_kernel,
        grid=(t // tm,),
        in_specs=[
            pl.BlockSpec((tm, d), row),
            pl.BlockSpec((1, d), const),
            pl.BlockSpec(w_in.shape, const),
            pl.BlockSpec((tm, LANES), lambda i: (i % n_pos_blk, 0)),
            pl.BlockSpec((tm, LANES), lambda i: (i % n_pos_blk, 0)),
        ],
        out_specs=out_specs,
        out_shape=out_shape,
        scratch_shapes=[pltpu.VMEM((3 * ATTN_WIDTH // LANES, tm, LANES), F32)],
        compiler_params=pltpu.CompilerParams(
            dimension_semantics=("parallel",), vmem_limit_bytes=VMEM_LIMIT),
        name="in_projection",
    )(xf, g, w_in, cos_t, sin_t)


def _conv_kernel(c_ref, w_ref, b_ref, lng_ref, lnb_ref, og_ref, o_ref, pad_ref):
    seq = c_ref.shape[1]
    n_slab = CONV_CH // LANES
    zeros = jnp.zeros((CONV_PAD, LANES), F32)
    for h in range(n_slab):
        pad_ref[h, 0:CONV_PAD, :] = zeros
        pad_ref[h, seq + CONV_PAD:seq + 2 * CONV_PAD, :] = zeros
        pad_ref[h, CONV_PAD:seq + CONV_PAD, :] = c_ref[0, :, h * LANES:(h + 1) * LANES]
    shift = CONV_PAD - CONV_KERNEL // 2

    def body(i, carry):
        base = pl.multiple_of(i * CONV_CHUNK, CONV_CHUNK)
        acc = []
        for h in range(n_slab):
            sl = slice(h * LANES, (h + 1) * LANES)
            a = jnp.zeros((CONV_CHUNK, LANES), F32)
            for k in range(CONV_KERNEL):
                a = a + pad_ref[h, pl.ds(base + (k + shift), CONV_CHUNK), :] * w_ref[k:k + 1, sl]
            acc.append(a + b_ref[:, sl])
        inv_ch = 1.0 / CONV_CH
        mu = sum(jnp.sum(a, axis=-1, keepdims=True) for a in acc) * inv_ch
        cen = [a - mu for a in acc]
        var = sum(jnp.sum(c * c, axis=-1, keepdims=True) for c in cen) * inv_ch
        rstd = lax.rsqrt(var + LN_EPS)
        ys = []
        for h, c in enumerate(cen):
            sl = slice(h * LANES, (h + 1) * LANES)
            y = c * rstd * lng_ref[:, sl] + lnb_ref[:, sl]
            ys.append(y / (1.0 + jnp.exp(-y)))
        ms = sum(jnp.sum(y * y, axis=-1, keepdims=True) for y in ys) * inv_ch
        rinv = lax.rsqrt(ms + NORM_EPS)
        for h, y in enumerate(ys):
            sl = slice(h * LANES, (h + 1) * LANES)
            o_ref[0, pl.ds(base, CONV_CHUNK), sl] = (y * rinv * og_ref[:, sl]).astype(BF16)
        return carry

    lax.fori_loop(0, seq // CONV_CHUNK, body, 0, unroll=2)


def _conformer_conv(c, w, b, lng, lnb, og):
    bsz, seq, ch = c.shape
    vec = pl.BlockSpec((1, ch), lambda i: (0, 0))
    return pl.pallas_call(
        _conv_kernel,
        grid=(bsz,),
        in_specs=[pl.BlockSpec((1, seq, ch), lambda i: (i, 0, 0)),
                  pl.BlockSpec(w.shape, lambda i: (0, 0)), vec, vec, vec, vec],
        out_specs=pl.BlockSpec((1, seq, ch), lambda i: (i, 0, 0)),
        out_shape=jax.ShapeDtypeStruct((bsz, seq, ch), BF16),
        scratch_shapes=[pltpu.VMEM((ch // LANES, seq + 2 * CONV_PAD, LANES), F32)],
        compiler_params=pltpu.CompilerParams(
            dimension_semantics=("parallel",), vmem_limit_bytes=VMEM_LIMIT),
        name="conformer_conv",
    )(c, w, b, lng, lnb, og)


def _band_attn_kernel(q_ref, k_ref, v_ref, o_ref, lse_ref, *, sub_len, q_blk, win):
    sq = ATTN_SUB
    row = lax.broadcasted_iota(jnp.int32, (2 * sq, win), 0) % sq
    col = lax.broadcasted_iota(jnp.int32, (2 * sq, win), 1)
    lane = lax.broadcasted_iota(jnp.int32, (sq, LANES), 1)
    is_a = lane < HEAD_DIM
    def chain(rows, ws, bias, hp):
        sl = slice(hp * LANES, (hp + 1) * LANES)
        qh = q_ref[rows, sl]
        kh = k_ref[pl.ds(ws, win), sl]
        zero = jnp.zeros_like(qh)
        q2 = jnp.concatenate([jnp.where(is_a, qh, zero), jnp.where(is_a, zero, qh)], axis=0)
        s = _dot_nt(q2, kh) + bias
        yield
        m = jnp.max(s, axis=1, keepdims=True)
        p = jnp.exp(s - m)
        l = jnp.sum(p, axis=1, keepdims=True)
        yield
        o2 = _dot(p.astype(BF16), v_ref[pl.ds(ws, win), sl]) / l
        lse2 = m + jnp.log(l)
        o_ref[rows, sl] = jnp.where(is_a, o2[:sq], o2[sq:]).astype(BF16)
        lse_ref[rows, sl] = jnp.where(is_a, lse2[:sq], lse2[sq:])
        yield

    chains = []
    for sub in range(q_blk // sq):
        q0 = pl.program_id(2) * q_blk + sub * sq
        ws = pl.multiple_of(jnp.clip(q0 - BAND, 0, sub_len - win), BAND)
        bias = jnp.where(jnp.abs(row - col + (q0 - ws)) <= BAND, 0.0, NEG).astype(F32)
        rows = slice(sub * sq, (sub + 1) * sq)
        chains += [chain(rows, ws, bias, hp) for hp in range(ATTN_WIDTH // LANES)]
    n_stage = 3
    for step in range(n_stage + len(chains) - 1):
        for ci in reversed(range(len(chains))):
            if 0 <= step - ci < n_stage:
                next(chains[ci])


def _band_attention(q, k, v):
    bsz, dil, sub_len, width = q.shape
    q_blk = min(256, sub_len)
    win = min(sub_len, ATTN_SUB + 2 * BAND)
    qmap = lambda b, r, i: (b, r, i, 0)
    kmap = lambda b, r, i: (b, r, 0, 0)
    return pl.pallas_call(
        functools.partial(_band_attn_kernel, sub_len=sub_len, q_blk=q_blk, win=win),
        grid=(bsz, dil, sub_len // q_blk),
        in_specs=[pl.BlockSpec((None, None, q_blk, width), qmap),
                  pl.BlockSpec((None, None, sub_len, width), kmap),
                  pl.BlockSpec((None, None, sub_len, width), kmap)],
        out_specs=[pl.BlockSpec((None, None, q_blk, width), qmap),
                   pl.BlockSpec((None, None, q_blk, width), qmap)],
        out_shape=[jax.ShapeDtypeStruct((bsz, dil, sub_len, width), BF16),
                   jax.ShapeDtypeStruct((bsz, dil, sub_len, width), F32)],
        compiler_params=pltpu.CompilerParams(
            dimension_semantics=("parallel", "parallel", "arbitrary"),
            vmem_limit_bytes=VMEM_LIMIT),
        name=f"band_attention_d{dil}",
    )(q, k, v)


def _memkv_kernel(m_ref, g_ref, wk_ref, wv_ref, k_ref, v_ref):
    h = _rms(m_ref[0], g_ref[...]).astype(BF16)
    k_ref[0] = _dot(h, wk_ref[...]).astype(BF16)
    v_ref[0] = _dot(h, wv_ref[...]).astype(BF16)


def _mem_kv(mem, g, wk, wv):
    bsz, m, d = mem.shape
    blk = pl.BlockSpec((1, m, d), lambda i: (i, 0, 0))
    const = lambda i: (0, 0)
    return pl.pallas_call(
        _memkv_kernel,
        grid=(bsz,),
        in_specs=[blk, pl.BlockSpec((1, d), const), pl.BlockSpec((d, d), const),
                  pl.BlockSpec((d, d), const)],
        out_specs=[blk, blk],
        out_shape=[jax.ShapeDtypeStruct((bsz, m, d), BF16)] * 2,
        compiler_params=pltpu.CompilerParams(dimension_semantics=("parallel",)),
        name="mem_kv",
    )(mem, g, wk, wv)


def _post_kernel(o1_ref, o2_ref, o3_ref, l1_ref, l2_ref, l3_ref, mc_ref, x_ref,
                 ag_ref, wout_ref, xg_ref, wq_ref, kx_ref, vx_ref, wo_ref,
                 mg_ref, wr_ref, br_ref,
                 x2_ref, h3_ref, route_ref, tot_ref,
                 slab_ref, *scratch):
    tm = x_ref.shape[0]
    d = x_ref.shape[1]
    n_grp = ATTN_WIDTH // LANES

    hm = POST_HALF
    lane = lax.broadcasted_iota(jnp.int32, (hm, LANES), 1)
    is_a = lane < HEAD_DIM
    w_rt = wr_ref[...]
    w_hi = w_rt.astype(BF16)
    w_lo = (w_rt - w_hi.astype(F32)).astype(BF16)

    def first_max(vals):
        vmax = jnp.max(vals, axis=1, keepdims=True)
        idx = jnp.min(jnp.where(vals == vmax, lane, LANES), axis=1, keepdims=True)
        return vmax, idx

    tot_box = [jnp.zeros((1, LANES), F32)]

    def chain(r0, mixed_ref, ox_ref):
        rs = slice(r0, r0 + hm)

        for bi, (o_ref, l_ref) in enumerate(((o2_ref, l2_ref), (o3_ref, l3_ref))):
            dil = o_ref.shape[1]
            n = hm // dil
            src = slice(r0 // dil, r0 // dil + n)
            base = bi * 2 * n_grp
            for r in range(dil):
                rows = pl.ds(r0 + r, n, stride=dil)
                for j in range(n_grp):
                    sl = slice(j * LANES, (j + 1) * LANES)
                    slab_ref[base + j, rows, :] = o_ref[0, r, src, sl].astype(F32)
                    slab_ref[base + n_grp + j, rows, :] = l_ref[0, r, src, sl]
        yield

        pieces = []
        ssq = jnp.zeros((hm, 1), F32)
        for hp in range(n_grp):
            sl = slice(hp * LANES, (hp + 1) * LANES)
            e1 = l1_ref[0, 0, rs, sl]
            e2 = slab_ref[n_grp + hp, rs, :]
            e3 = slab_ref[3 * n_grp + hp, rs, :]
            mx = jnp.maximum(jnp.maximum(e1, e2), e3)
            w1, w2, w3 = jnp.exp(e1 - mx), jnp.exp(e2 - mx), jnp.exp(e3 - mx)
            num = (w1 * o1_ref[0, 0, rs, sl].astype(F32) + w2 * slab_ref[hp, rs, :]
                   + w3 * slab_ref[2 * n_grp + hp, rs, :])
            a = num / (w1 + w2 + w3)
            pieces.append(a)
            ssq = ssq + jnp.sum(a * a, axis=1, keepdims=True)
        inv = lax.rsqrt(ssq * (1.0 / ATTN_WIDTH) + NORM_EPS)
        mixed_ref[:, 0:CONV_CH] = mc_ref[rs, :]
        for hp, a in enumerate(pieces):
            sl = slice(hp * LANES, (hp + 1) * LANES)
            mixed_ref[:, CONV_CH + hp * LANES:CONV_CH + (hp + 1) * LANES] = (
                a * inv * ag_ref[:, sl]).astype(BF16)
        yield
        x1 = x_ref[rs, :] + _dot(mixed_ref[...], wout_ref[...])
        yield

        hd = d // MEM_HEADS
        qx = (_dot(_rms(x1, xg_ref[...]).astype(BF16), wq_ref[...]) * (hd ** -0.5)).astype(BF16)
        yield
        for h in range(MEM_HEADS):
            sl = slice(h * hd, (h + 1) * hd)
            s = _dot_nt(qx[:, sl], kx_ref[0, :, sl])
            m = jnp.max(s, axis=1, keepdims=True)
            p = jnp.exp(s - m)
            l = jnp.sum(p, axis=1, keepdims=True)
            ox_ref[:, sl] = (_dot(p.astype(BF16), vx_ref[0, :, sl]) / l).astype(BF16)
        yield
        x2 = x1 + _dot(ox_ref[...], wo_ref[...])
        x2_ref[rs, :] = x2
        yield

        h3 = _rms(x2, mg_ref[...])
        h3_ref[rs, :] = h3.astype(BF16)
        h_hi = h3.astype(BF16)
        h_lo = (h3 - h_hi.astype(F32)).astype(BF16)
        lg = _dot(h_hi, w_hi) + (_dot(h_hi, w_lo) + _dot(h_lo, w_hi)) + br_ref[...]

        is_group = lane < N_GROUPS
        gmax, gidx = first_max(jnp.where(is_group, lg, NEG))
        p_g = 1.0 / jnp.sum(jnp.where(is_group, jnp.exp(lg - gmax), 0.0), axis=1, keepdims=True)
        lo = N_GROUPS + EXPERTS_PER_GROUP * gidx
        el = jnp.where((lane >= lo) & (lane < lo + EXPERTS_PER_GROUP), lg, NEG)
        v1, i1 = first_max(el)
        v2, i2 = first_max(jnp.where(lane == i1, NEG, el))
        t2 = jnp.exp(v2 - v1)
        g1 = p_g / (1.0 + t2)
        g2 = g1 * t2
        e1 = i1 - N_GROUPS
        e2 = i2 - N_GROUPS

        hits = jnp.where((lane == e1) | (lane == e2), 1.0, 0.0)
        tot_box[0] = tot_box[0] + jnp.sum(hits, axis=0, keepdims=True)

        route = jnp.zeros((hm, LANES), F32)
        for pos, val in enumerate((e1.astype(F32), e2.astype(F32), g1, g2)):
            route = jnp.where(lane == pos, val, route)
        route_ref[rs, :] = route
        yield

    chains = [chain(i * hm, scratch[2 * i], scratch[2 * i + 1]) for i in range(tm // hm)]
    n_stage = 7
    for step in range(n_stage + len(chains) - 1):
        for ci, ch in enumerate(chains):
            if 0 <= step - ci < n_stage:
                next(ch)
    tot_ref[0] = tot_box[0]


def _post_mix(o_list, lse_list, mixc, xf, ag, w_out, xg, w_xq, kx, vx, w_xo, mg, w_rt, b_rt, seq):
    t, d = xf.shape
    tm = TM_POST
    blk_per_seq = seq // tm
    row = lambda i: (i, 0)
    const = lambda i: (0, 0)
    bmap = lambda i: (i // blk_per_seq, 0, 0)
    mlen = kx.shape[1]
    cls = lambda i: (i // blk_per_seq, 0, i % blk_per_seq, 0)
    wide = [pl.BlockSpec((1, dil, tm // dil, ATTN_WIDTH), cls) for dil in DILATIONS]
    narrow = pl.BlockSpec((tm, LANES), row)
    full = pl.BlockSpec((tm, d), row)
    mat = pl.BlockSpec((d, d), const)
    n_slab = 4 * (ATTN_WIDTH // LANES)
    return pl.pallas_call(
        _post_kernel,
        grid=(t // tm,),
        in_specs=[*wide, *wide,
                  pl.BlockSpec((tm, CONV_CH), row), full,
                  pl.BlockSpec((1, ATTN_WIDTH), const), mat,
                  pl.BlockSpec((1, d), const), mat,
                  pl.BlockSpec((1, mlen, d), bmap), pl.BlockSpec((1, mlen, d), bmap), mat,
                  pl.BlockSpec((1, d), const), pl.BlockSpec((d, LANES), const),
                  pl.BlockSpec((1, LANES), const)],
        out_specs=[full, full, narrow, pl.BlockSpec((1, 1, LANES), lambda i: (i, 0, 0))],
        out_shape=[jax.ShapeDtypeStruct((t, d), F32), jax.ShapeDtypeStruct((t, d), BF16),
                   jax.ShapeDtypeStruct((t, LANES), F32),
                   jax.ShapeDtypeStruct((t // tm, 1, LANES), F32)],
        scratch_shapes=[pltpu.VMEM((n_slab, tm, LANES), F32)]
        + [pltpu.VMEM((POST_HALF, d), BF16)] * (2 * (tm // POST_HALF)),
        compiler_params=pltpu.CompilerParams(
            dimension_semantics=("parallel",), vmem_limit_bytes=VMEM_LIMIT),
        name="post_mix",
    )(*o_list, *lse_list, mixc, xf, ag, w_out, xg, w_xq, kx, vx, w_xo, mg, w_rt, b_rt)


ROW_ALIGN = 8
RUN_PIECES = tuple(2 ** k for k in range(9, 2, -1))
TILE_PIECES = (1024,) + RUN_PIECES
LOCAL_ROWS = 2 * TM_ROW + ROW_ALIGN * N_EXPERTS


def _run_copies(tab_ref, local_ref, global_ref, sem, to_global):
    def body(e, carry):
        ls = tab_ref[0, 0, e]
        gs = tab_ref[0, 0, N_EXPERTS + e]
        n = tab_ref[0, 0, 2 * N_EXPERTS + e]
        for size in RUN_PIECES:
            done = (n // (2 * size)) * (2 * size)

            @pl.when((n & size) != 0)
            def _():
                loc = local_ref.at[pl.ds(pl.multiple_of(ls + done, ROW_ALIGN), size)]
                glo = global_ref.at[pl.ds(pl.multiple_of(gs + done, ROW_ALIGN), size)]
                src, dst = (loc, glo) if to_global else (glo, loc)
                pltpu.make_async_copy(src, dst, sem).start()
        return carry

    lax.fori_loop(0, N_EXPERTS, body, 0)


def _run_waits(total, local_ref, global_ref, sem):
    for size in TILE_PIECES:
        @pl.when((total & size) != 0)
        def _():
            pltpu.make_async_copy(local_ref.at[pl.ds(0, size)], global_ref.at[pl.ds(0, size)],
                                  sem).wait()


def _local_positions(route, lstart):
    tm = route.shape[0]
    lane = lax.broadcasted_iota(jnp.int32, (tm, LANES), 1)
    rr = lax.broadcasted_iota(jnp.int32, (tm, tm), 0)
    cc = lax.broadcasted_iota(jnp.int32, (tm, tm), 1)
    ltri = jnp.where(cc < rr, 1.0, 0.0).astype(BF16)
    oh1 = jnp.where(lane == route[:, 0:1].astype(jnp.int32), 1.0, 0.0)
    oh2 = jnp.where(lane == route[:, 1:2].astype(jnp.int32), 1.0, 0.0)
    pre1 = _dot(ltri, oh1.astype(BF16))
    pre2 = _dot(ltri, oh2.astype(BF16))
    tot1 = jnp.sum(oh1, axis=0, keepdims=True)
    lp1 = jnp.sum(oh1 * (lstart + pre1), axis=1, keepdims=True)
    lp2 = jnp.sum(oh2 * (lstart + tot1 + pre2), axis=1, keepdims=True)
    return lp1, lp2


def _dispatch_kernel(ps_ref, pe_ref, tab_ref, ls_ref, route_ref, h_ref, xout_ref, lp_ref,
                     zero_ref, xs_ref, pend_ref, sems):
    tm = h_ref.shape[0]
    bm = zero_ref.shape[0]
    sem = sems.at[2]

    @pl.when(pl.program_id(0) == 0)
    def _():
        zero_ref[...] = jnp.zeros_like(zero_ref)

        def pad_copy(e):
            last = pl.multiple_of(pe_ref[e] - bm, bm)
            return pltpu.make_async_copy(zero_ref, xout_ref.at[pl.ds(last, bm)], sem)

        def nonempty(e):
            return pe_ref[e] > ps_ref[e]

        def start(e, carry):
            pl.when(nonempty(e))(lambda: pad_copy(e).start())
            return carry

        def wait(e, carry):
            pl.when(nonempty(e))(lambda: pad_copy(e).wait())
            return carry

        lax.fori_loop(0, N_EXPERTS, start, 0)
        lax.fori_loop(0, N_EXPERTS, wait, 0)

        def tail_copy(b):
            return pltpu.make_async_copy(
                zero_ref, xout_ref.at[pl.ds(pl.multiple_of(b * bm, bm), bm)], sem)

        def tail_start(b, carry):
            tail_copy(b).start()
            return carry

        def tail_wait(b, carry):
            tail_copy(b).wait()
            return carry

        first_tail = pe_ref[N_EXPERTS - 1] // bm
        n_blk = xout_ref.shape[0] // bm
        lax.fori_loop(first_tail, n_blk, tail_start, 0)
        lax.fori_loop(first_tail, n_blk, tail_wait, 0)

    i = pl.program_id(0)
    n_step = pl.num_programs(0)
    slot = i % 2

    def drain(s):
        _run_waits(pend_ref[s], xs_ref.at[s], xout_ref, sems.at[s])

    @pl.when(i >= 2)
    def _():
        drain(slot)

    lp1, lp2 = _local_positions(route_ref[...], ls_ref[0])
    lane = lax.broadcasted_iota(jnp.int32, (tm, LANES), 1)
    lp_ref[...] = jnp.where(lane == 0, lp1, jnp.where(lane == 1, lp2, 0.0))
    row1 = jnp.broadcast_to(lp1, (tm, LANES)).T[0:1, :].astype(jnp.int32)
    row2 = jnp.broadcast_to(lp2, (tm, LANES)).T[0:1, :].astype(jnp.int32)
    slot_id = lax.broadcasted_iota(jnp.int32, (LOCAL_ROWS, tm), 0)
    perm = jnp.where((slot_id == row1) | (slot_id == row2), 1.0, 0.0).astype(BF16)
    xs_ref[slot] = _dot(perm, h_ref[...])
    _run_copies(tab_ref, xs_ref.at[slot], xout_ref, sems.at[slot], to_global=True)
    pend_ref[slot] = tab_ref[0, 0, 3 * N_EXPERTS]

    @pl.when(i == n_step - 1)
    def _():
        drain(slot)

        @pl.when(n_step > 1)
        def _():
            drain(1 - slot)


def _dispatch(pstarts, pends, tab, lstart, route, h3, n_rows):
    t, d = h3.shape
    tm = TM_ROW
    tile = lambda i, ps, pe: (i, 0, 0)
    row = lambda i, ps, pe: (i, 0)
    return pl.pallas_call(
        _dispatch_kernel,
        grid_spec=pltpu.PrefetchScalarGridSpec(
            num_scalar_prefetch=2,
            grid=(t // tm,),
            in_specs=[pl.BlockSpec((1, 1, LANES), tile, memory_space=pltpu.SMEM),
                      pl.BlockSpec((1, 1, LANES), tile),
                      pl.BlockSpec((tm, LANES), row),
                      pl.BlockSpec((tm, d), row)],
            out_specs=[pl.BlockSpec(memory_space=pl.ANY), pl.BlockSpec((tm, LANES), row)],
            scratch_shapes=[pltpu.VMEM((BM_EXPERT, d), F32), pltpu.VMEM((2, LOCAL_ROWS, d), F32),
                            pltpu.SMEM((2,), jnp.int32), pltpu.SemaphoreType.DMA((3,))]),
        out_shape=[jax.ShapeDtypeStruct((n_rows, d), F32),
                   jax.ShapeDtypeStruct((t, LANES), F32)],
        compiler_params=pltpu.CompilerParams(
            dimension_semantics=("arbitrary",), vmem_limit_bytes=VMEM_LIMIT),
        name="moe_dispatch",
    )(pstarts, pends, tab, lstart, route, h3)


def _expert_kernel(be_ref, nu_ref, x_ref, w1_ref, w3_ref, w2_ref, y_ref, w1b, w3b, w2b):
    i = pl.program_id(0)
    prev = be_ref[jnp.maximum(i - 1, 0)]

    @pl.when((i == 0) | (be_ref[i] != prev))
    def _():
        w1b[...] = w1_ref[0].astype(BF16)
        w3b[...] = w3_ref[0].astype(BF16)
        w2b[...] = w2_ref[0].astype(BF16)

    @pl.when(i < nu_ref[0])
    def _():
        xb = x_ref[...].astype(BF16)
        a = _dot(xb, w1b[...])
        g = _dot(xb, w3b[...])
        act = (a / (1.0 + jnp.exp(-a)) * g).astype(BF16)
        y_ref[...] = _dot(act, w2b[...])

    @pl.when(i >= nu_ref[0])
    def _():
        y_ref[...] = jnp.zeros_like(y_ref)


def _experts(blk_e, n_used, x_rows, w1, w3, w2):
    p, d = x_rows.shape
    ff = w1.shape[2]
    bm = BM_EXPERT
    wmap = lambda i, be, nu: (be[i], 0, 0)
    return pl.pallas_call(
        _expert_kernel,
        grid_spec=pltpu.PrefetchScalarGridSpec(
            num_scalar_prefetch=2,
            grid=(p // bm,),
            in_specs=[pl.BlockSpec((bm, d), lambda i, be, nu: (jnp.minimum(i, nu[0] - 1), 0)),
                      pl.BlockSpec((1, d, ff), wmap),
                      pl.BlockSpec((1, d, ff), wmap),
                      pl.BlockSpec((1, ff, d), wmap)],
            out_specs=pl.BlockSpec((bm, d), lambda i, be, nu: (i, 0)),
            scratch_shapes=[pltpu.VMEM((d, ff), BF16), pltpu.VMEM((d, ff), BF16),
                            pltpu.VMEM((ff, d), BF16)]),
        out_shape=jax.ShapeDtypeStruct((p, d), F32),
        compiler_params=pltpu.CompilerParams(
            dimension_semantics=("arbitrary",), vmem_limit_bytes=VMEM_LIMIT),
        name="moe_experts",
    )(blk_e, n_used, x_rows, w1, w3, w2)


def _combine_kernel(tab_ref, nxt_ref, x2_ref, route_ref, lp_ref, g_ref, y_ref, o_ref, ys_ref, sems):
    tm = x2_ref.shape[0]
    i = pl.program_id(0)
    n_step = pl.num_programs(0)
    slot = i % 2

    @pl.when(i == 0)
    def _():
        ys_ref[...] = jnp.zeros_like(ys_ref)
        _run_copies(tab_ref, ys_ref.at[0], y_ref, sems.at[0], to_global=False)

    @pl.when(i + 1 < n_step)
    def _():
        _run_copies(nxt_ref, ys_ref.at[1 - slot], y_ref, sems.at[1 - slot], to_global=False)

    _run_waits(tab_ref[0, 0, 3 * N_EXPERTS], ys_ref.at[slot], y_ref, sems.at[slot])

    route = route_ref[...]
    lp = lp_ref[...]
    slot_id = lax.broadcasted_iota(jnp.int32, (tm, LOCAL_ROWS), 1)
    sel = (jnp.where(slot_id == lp[:, 0:1].astype(jnp.int32), route[:, 2:3], 0.0)
           + jnp.where(slot_id == lp[:, 1:2].astype(jnp.int32), route[:, 3:4], 0.0))
    x3 = x2_ref[...] + _dot(sel.astype(BF16), ys_ref[slot].astype(BF16))
    o_ref[...] = _rms(x3, g_ref[...])


def _combine(tab, x2, route, lp, g, y_rows):
    t, d = x2.shape
    tm = TM_ROW
    n_tile = t // tm
    row = lambda i: (i, 0)
    return pl.pallas_call(
        _combine_kernel,
        grid=(n_tile,),
        in_specs=[pl.BlockSpec((1, 1, LANES), lambda i: (i, 0, 0), memory_space=pltpu.SMEM),
                  pl.BlockSpec((1, 1, LANES), lambda i: (jnp.minimum(i + 1, n_tile - 1), 0, 0),
                               memory_space=pltpu.SMEM),
                  pl.BlockSpec((tm, d), row),
                  pl.BlockSpec((tm, LANES), row),
                  pl.BlockSpec((tm, LANES), row),
                  pl.BlockSpec((1, d), lambda i: (0, 0)),
                  pl.BlockSpec(memory_space=pl.ANY)],
        out_specs=pl.BlockSpec((tm, d), row),
        out_shape=jax.ShapeDtypeStruct((t, d), F32),
        scratch_shapes=[pltpu.VMEM((2, LOCAL_ROWS, d), F32), pltpu.SemaphoreType.DMA((2,))],
        compiler_params=pltpu.CompilerParams(
            dimension_semantics=("arbitrary",), vmem_limit_bytes=VMEM_LIMIT),
        name="moe_combine",
    )(tab, tab, x2, route, lp, g, y_rows)


def kernel(x, mem, positions, mix_norm_g, w_in, conv_dw_w, conv_dw_b, conv_ln_g, conv_ln_b,
           conv_out_g, attn_out_g, w_out, xattn_norm_g, mem_norm_g, w_xq, w_xk, w_xv, w_xo,
           moe_norm_g, w_group, b_group, w_router, b_router, w1, w3, w2, final_norm_g):
    bsz, seq, d = x.shape
    assert w_in.shape[0] == 1, "single-layer encoder only"
    l = 0
    t = bsz * seq
    vec = lambda a: a.reshape(1, -1)
    cos_t, sin_t = _rope_tables(positions)
    xf = x.reshape(t, d)
    c, *qkv = _in_projection(xf, vec(mix_norm_g[l]), w_in[l].astype(BF16), cos_t, sin_t, bsz, seq)
    mixc = _conformer_conv(c.reshape(bsz, seq, CONV_CH), conv_dw_w[l], vec(conv_dw_b[l]),
                           vec(conv_ln_g[l]), vec(conv_ln_b[l]), vec(conv_out_g[l]))
    branches = [_band_attention(*qkv[3 * i:3 * i + 3]) for i in range(len(DILATIONS))]
    kx, vx = _mem_kv(mem, vec(mem_norm_g[l]), w_xk[l].astype(BF16), w_xv[l].astype(BF16))
    pad = LANES - N_GROUPS - N_EXPERTS
    w_rt = jnp.pad(jnp.concatenate([w_group[l], w_router[l]], axis=1), ((0, 0), (0, pad)))
    b_rt = jnp.pad(jnp.concatenate([b_group[l], b_router[l]]), (0, pad)).reshape(1, LANES)
    x2, h3, route, tile_tot = _post_mix(
        [o for o, _ in branches], [s for _, s in branches], mixc.reshape(t, CONV_CH), xf,
        vec(attn_out_g[l]), w_out[l].astype(BF16), vec(xattn_norm_g[l]), w_xq[l].astype(BF16),
        kx, vx, w_xo[l].astype(BF16), vec(moe_norm_g[l]), w_rt, b_rt, seq)

    assert TM_ROW == TM_POST
    bm = BM_EXPERT
    n_tile = t // TM_ROW
    tt = tile_tot[:, 0, :N_EXPERTS].astype(jnp.int32)
    tt = (tt + ROW_ALIGN - 1) // ROW_ALIGN * ROW_ALIGN
    lstart = jnp.cumsum(tt, axis=1) - tt
    before = jnp.cumsum(tt, axis=0) - tt
    pcounts = (jnp.sum(tt, axis=0) + bm - 1) // bm * bm
    pends = jnp.cumsum(pcounts).astype(jnp.int32)
    pstarts = pends - pcounts
    n_blk = (2 * t + (ROW_ALIGN - 1) * n_tile * N_EXPERTS) // bm + N_EXPERTS
    n_used = pends[-1:] // bm
    blk = jnp.minimum(jnp.arange(n_blk, dtype=jnp.int32), n_used - 1)
    blk_e = jnp.sum((blk[:, None] * bm >= pends[None, :]).astype(jnp.int32), axis=1)
    tile_rows = jnp.sum(tt, axis=1, keepdims=True)
    tab = jnp.concatenate([lstart, pstarts[None, :] + before, tt, tile_rows,
                           jnp.zeros((n_tile, N_EXPERTS - 1), jnp.int32)], axis=1)[:, None, :]
    lstart_v = jnp.pad(lstart.astype(F32), ((0, 0), (0, LANES - N_EXPERTS)))[:, None, :]

    x_rows, lp = _dispatch(pstarts, pends, tab, lstart_v, route, h3, n_blk * bm)
    y_rows = _experts(blk_e, n_used, x_rows, w1[l], w3[l], w2[l])
    out = _combine(tab, x2, route, lp, vec(final_norm_g), y_rows)
    return out.reshape(bsz, seq, d)
```

```python
import functools

import jax
import jax.numpy as jnp
from jax import lax
from jax.experimental import pallas as pl
from jax.experimental.pallas import tpu as pltpu

HEAD_DIM = 64
CONV_CH = 256
ATTN_HEADS = 12
ATTN_WIDTH = ATTN_HEADS * HEAD_DIM
CONV_KERNEL = 31
CONV_PAD = 16
BAND = 64
DILATIONS = (1, 4, 16)
ROPE_THETA = 10000.0
MEM_HEADS = 4
N_GROUPS = 4
EXPERTS_PER_GROUP = 8
N_EXPERTS = N_GROUPS * EXPERTS_PER_GROUP
NORM_EPS = 1e-6
LN_EPS = 1e-5
LANES = 128
NEG = -1e30
LOG2_E = 1.4426950408889634
LN_2 = 0.6931471805599453

TM_IN = 512
TM_POST = 512
POST_HALF = 256
TM_ROW = 512
BM_EXPERT = 512
CONV_CHUNK = 128
ATTN_SUB = 128
ATTN_Q_BLOCK = 512
VMEM_LIMIT = 56 * 1024 * 1024

F32 = jnp.float32
BF16 = jnp.bfloat16


def _rms(x, g):
    return x * lax.rsqrt(jnp.mean(x * x, axis=-1, keepdims=True) + NORM_EPS) * g


def _dot(a, b):
    return jnp.dot(a, b, preferred_element_type=F32)


def _dot_nt(a, b):
    return lax.dot_general(a, b, (((1,), (1,)), ((), ())), preferred_element_type=F32)


def _rope_kernel(pos_ref, inv_ref, sign_ref, cos_ref, sin_ref):
    ang = pos_ref[...].astype(F32) * inv_ref[...]
    cos_ref[...] = jnp.cos(ang)
    sin_ref[...] = jnp.sin(ang) * sign_ref[...]


def _rope_tables(positions):
    s = positions.shape[0]
    half = HEAD_DIM // 2
    inv = 1.0 / (ROPE_THETA ** (jnp.arange(half, dtype=F32) * (2.0 / HEAD_DIM)))
    inv = jnp.tile(inv, LANES // half)[None, :]
    sign = jnp.tile(jnp.concatenate([-jnp.ones((half,), F32), jnp.ones((half,), F32)]),
                    LANES // HEAD_DIM)[None, :]
    return pl.pallas_call(
        _rope_kernel,
        out_shape=(jax.ShapeDtypeStruct((s, LANES), F32), jax.ShapeDtypeStruct((s, LANES), F32)),
        name="rope_tables",
    )(positions.reshape(s, 1), inv, sign)


def _inproj_kernel(x_ref, g_ref, w_ref, cos_ref, sin_ref, c_ref, *rest):
    out_refs, slab_ref = rest[:-1], rest[-1]
    tm = x_ref.shape[0]
    n_grp = ATTN_WIDTH // LANES
    h = _rms(x_ref[...], g_ref[...]).astype(BF16)
    u = _dot(h, w_ref[:, 0:2 * CONV_CH])
    c_ref[...] = u[:, :CONV_CH] / (1.0 + jnp.exp(-u[:, CONV_CH:]))
    cos = cos_ref[...]
    sin = sin_ref[...]
    lane = lax.broadcasted_iota(jnp.int32, cos.shape, 1)
    first_half = (lane % HEAD_DIM) < (HEAD_DIM // 2)
    off = 2 * CONV_CH
    for which, scale in enumerate((HEAD_DIM ** -0.5 * LOG2_E, 1.0, None)):
        u = _dot(h, w_ref[:, off:off + ATTN_WIDTH])
        off += ATTN_WIDTH
        for j in range(n_grp):
            xs = u[:, j * LANES:(j + 1) * LANES]
            if scale is not None:
                partner = jnp.where(first_half, pltpu.roll(xs, LANES - 32, 1),
                                    pltpu.roll(xs, 32, 1))
                xs = (xs * cos + partner * sin) * scale
            slab_ref[which * n_grp + j] = xs
        for di, dil in enumerate(DILATIONS):
            ref = out_refs[3 * di + which]
            n = tm // dil
            for r in range(dil):
                for j in range(n_grp):
                    idx = which * n_grp + j
                    rows = slab_ref[idx] if dil == 1 else slab_ref[idx, pl.ds(r, n, stride=dil), :]
                    ref[0, r, :, j * LANES:(j + 1) * LANES] = rows.astype(BF16)


def _in_projection(xf, g, w_in, cos_t, sin_t, bsz, seq):
    t, d = xf.shape
    tm = TM_IN
    n_pos_blk = seq // tm
    row = lambda i: (i, 0)
    const = lambda i: (0, 0)
    cls = lambda i: (i // n_pos_blk, 0, i % n_pos_blk, 0)
    out_specs = [pl.BlockSpec((tm, CONV_CH), row)]
    out_shape = [jax.ShapeDtypeStruct((t, CONV_CH), F32)]
    for dil in DILATIONS:
        for _ in range(3):
            out_specs.append(pl.BlockSpec((1, dil, tm // dil, ATTN_WIDTH), cls))
            out_shape.append(jax.ShapeDtypeStruct((bsz, dil, seq // dil, ATTN_WIDTH), BF16))
    return pl.pallas_call(
        _inproj_kernel,
        grid=(t // tm,),
        in_specs=[
            pl.BlockSpec((tm, d), row),
            pl.BlockSpec((1, d), const),
            pl.BlockSpec(w_in.shape, const),
            pl.BlockSpec((tm, LANES), lambda i: (i % n_pos_blk, 0)),
            pl.BlockSpec((tm, LANES), lambda i: (i % n_pos_blk, 0)),
        ],
        out_specs=out_specs,
        out_shape=out_shape,
        scratch_shapes=[pltpu.VMEM((3 * ATTN_WIDTH // LANES, tm, LANES), F32)],
        compiler_params=pltpu.CompilerParams(
            dimension_semantics=("parallel",), vmem_limit_bytes=VMEM_LIMIT),
        name="in_projection",
    )(xf, g, w_in, cos_t, sin_t)


def _conv_kernel(c_ref, w_ref, b_ref, lng_ref, lnb_ref, og_ref, o_ref, pad_ref):
    seq = c_ref.shape[1]
    n_slab = CONV_CH // LANES
    zeros = jnp.zeros((CONV_PAD, LANES), F32)
    for h in range(n_slab):
        pad_ref[h, 0:CONV_PAD, :] = zeros
        pad_ref[h, seq + CONV_PAD:seq + 2 * CONV_PAD, :] = zeros
        pad_ref[h, CONV_PAD:seq + CONV_PAD, :] = c_ref[0, :, h * LANES:(h + 1) * LANES]
    shift = CONV_PAD - CONV_KERNEL // 2

    def body(i, carry):
        base = pl.multiple_of(i * CONV_CHUNK, CONV_CHUNK)
        acc = []
        for h in range(n_slab):
            sl = slice(h * LANES, (h + 1) * LANES)
            a = jnp.zeros((CONV_CHUNK, LANES), F32)
            for k in range(CONV_KERNEL):
                a = a + pad_ref[h, pl.ds(base + (k + shift), CONV_CHUNK), :] * w_ref[k:k + 1, sl]
            acc.append(a + b_ref[:, sl])
        inv_ch = 1.0 / CONV_CH
        mu = sum(jnp.sum(a, axis=-1, keepdims=True) for a in acc) * inv_ch
        cen = [a - mu for a in acc]
        var = sum(jnp.sum(c * c, axis=-1, keepdims=True) for c in cen) * inv_ch
        rstd = lax.rsqrt(var + LN_EPS)
        ys = []
        for h, c in enumerate(cen):
            sl = slice(h * LANES, (h + 1) * LANES)
            y = c * rstd * lng_ref[:, sl] + lnb_ref[:, sl]
            ys.append(y / (1.0 + jnp.exp(-y)))
        ms = sum(jnp.sum(y * y, axis=-1, keepdims=True) for y in ys) * inv_ch
        rinv = lax.rsqrt(ms + NORM_EPS)
        for h, y in enumerate(ys):
            sl = slice(h * LANES, (h + 1) * LANES)
            o_ref[0, pl.ds(base, CONV_CHUNK), sl] = (y * rinv * og_ref[:, sl]).astype(BF16)
        return carry

    lax.fori_loop(0, seq // CONV_CHUNK, body, 0, unroll=2)


def _conformer_conv(c, w, b, lng, lnb, og):
    bsz, seq, ch = c.shape
    vec = pl.BlockSpec((1, ch), lambda i: (0, 0))
    return pl.pallas_call(
        _conv_kernel,
        grid=(bsz,),
        in_specs=[pl.BlockSpec((1, seq, ch), lambda i: (i, 0, 0)),
                  pl.BlockSpec(w.shape, lambda i: (0, 0)), vec, vec, vec, vec],
        out_specs=pl.BlockSpec((1, seq, ch), lambda i: (i, 0, 0)),
        out_shape=jax.ShapeDtypeStruct((bsz, seq, ch), BF16),
        scratch_shapes=[pltpu.VMEM((ch // LANES, seq + 2 * CONV_PAD, LANES), F32)],
        compiler_params=pltpu.CompilerParams(
            dimension_semantics=("parallel",), vmem_limit_bytes=VMEM_LIMIT),
        name="conformer_conv",
    )(c, w, b, lng, lnb, og)


def _band_attn_kernel(q_ref, k_ref, v_ref, o_ref, lse_ref, *, sub_len, q_blk, win):
    sq = ATTN_SUB
    row = lax.broadcasted_iota(jnp.int32, (2 * sq, win), 0) % sq
    col = lax.broadcasted_iota(jnp.int32, (2 * sq, win), 1)
    lane = lax.broadcasted_iota(jnp.int32, (sq, LANES), 1)
    is_a = lane < HEAD_DIM
    def chain(rows, ws, bias, hp):
        sl = slice(hp * LANES, (hp + 1) * LANES)
        qh = q_ref[rows, sl]
        kh = k_ref[pl.ds(ws, win), sl]
        zero = jnp.zeros_like(qh)
        q2 = jnp.concatenate([jnp.where(is_a, qh, zero), jnp.where(is_a, zero, qh)], axis=0)
        s = _dot_nt(q2, kh) + bias
        yield
        m = jnp.max(s, axis=1, keepdims=True)
        p = jnp.exp2(s - m)
        l = jnp.sum(p, axis=1, keepdims=True)
        yield
        o2 = _dot(p.astype(BF16), v_ref[pl.ds(ws, win), sl]) / l
        lse2 = m * LN_2 + jnp.log(l)
        o_ref[rows, sl] = jnp.where(is_a, o2[:sq], o2[sq:]).astype(BF16)
        lse_ref[rows, sl] = jnp.where(is_a, lse2[:sq], lse2[sq:])
        yield

    chains = []
    for sub in range(q_blk // sq):
        q0 = pl.program_id(2) * q_blk + sub * sq
        ws = pl.multiple_of(jnp.clip(q0 - BAND, 0, sub_len - win), BAND)
        bias = jnp.where(jnp.abs(row - col + (q0 - ws)) <= BAND, 0.0, NEG).astype(F32)
        rows = slice(sub * sq, (sub + 1) * sq)
        chains += [chain(rows, ws, bias, hp) for hp in range(ATTN_WIDTH // LANES)]
    n_stage = 3
    for step in range(n_stage + len(chains) - 1):
        for ci in reversed(range(len(chains))):
            if 0 <= step - ci < n_stage:
                next(chains[ci])


def _band_attention(q, k, v):
    bsz, dil, sub_len, width = q.shape
    q_blk = min(ATTN_Q_BLOCK, sub_len)
    win = min(sub_len, ATTN_SUB + 2 * BAND)
    qmap = lambda b, r, i: (b, r, i, 0)
    kmap = lambda b, r, i: (b, r, 0, 0)
    return pl.pallas_call(
        functools.partial(_band_attn_kernel, sub_len=sub_len, q_blk=q_blk, win=win),
        grid=(bsz, dil, sub_len // q_blk),
        in_specs=[pl.BlockSpec((None, None, q_blk, width), qmap),
                  pl.BlockSpec((None, None, sub_len, width), kmap),
                  pl.BlockSpec((None, None, sub_len, width), kmap)],
        out_specs=[pl.BlockSpec((None, None, q_blk, width), qmap),
                   pl.BlockSpec((None, None, q_blk, width), qmap)],
        out_shape=[jax.ShapeDtypeStruct((bsz, dil, sub_len, width), BF16),
                   jax.ShapeDtypeStruct((bsz, dil, sub_len, width), F32)],
        compiler_params=pltpu.CompilerParams(
            dimension_semantics=("parallel", "parallel", "arbitrary"),
            vmem_limit_bytes=VMEM_LIMIT),
        name=f"band_attention_d{dil}",
    )(q, k, v)


def _memkv_kernel(m_ref, g_ref, wk_ref, wv_ref, k_ref, v_ref):
    h = _rms(m_ref[0], g_ref[...]).astype(BF16)
    k_ref[0] = _dot(h, wk_ref[...]).astype(BF16)
    v_ref[0] = _dot(h, wv_ref[...]).astype(BF16)


def _mem_kv(mem, g, wk, wv):
    bsz, m, d = mem.shape
    blk = pl.BlockSpec((1, m, d), lambda i: (i, 0, 0))
    const = lambda i: (0, 0)
    return pl.pallas_call(
        _memkv_kernel,
        grid=(bsz,),
        in_specs=[blk, pl.BlockSpec((1, d), const), pl.BlockSpec((d, d), const),
                  pl.BlockSpec((d, d), const)],
        out_specs=[blk, blk],
        out_shape=[jax.ShapeDtypeStruct((bsz, m, d), BF16)] * 2,
        compiler_params=pltpu.CompilerParams(dimension_semantics=("parallel",)),
        name="mem_kv",
    )(mem, g, wk, wv)


def _post_kernel(o1_ref, o2_ref, o3_ref, l1_ref, l2_ref, l3_ref, mc_ref, x_ref,
                 ag_ref, wout_ref, xg_ref, wq_ref, kx_ref, vx_ref, wo_ref,
                 mg_ref, wr_ref, br_ref,
                 x2_ref, h3_ref, route_ref, tot_ref,
                 slab_ref, *scratch):
    tm = x_ref.shape[0]
    d = x_ref.shape[1]
    n_grp = ATTN_WIDTH // LANES

    hm = POST_HALF
    lane = lax.broadcasted_iota(jnp.int32, (hm, LANES), 1)
    is_a = lane < HEAD_DIM
    w_rt = wr_ref[...]
    w_hi = w_rt.astype(BF16)
    w_lo = (w_rt - w_hi.astype(F32)).astype(BF16)

    def first_max(vals):
        vmax = jnp.max(vals, axis=1, keepdims=True)
        idx = jnp.min(jnp.where(vals == vmax, lane, LANES), axis=1, keepdims=True)
        return vmax, idx

    tot_box = [jnp.zeros((1, LANES), F32)]

    def chain(r0, mixed_ref, ox_ref):
        rs = slice(r0, r0 + hm)

        for bi, (o_ref, l_ref) in enumerate(((o2_ref, l2_ref), (o3_ref, l3_ref))):
            dil = o_ref.shape[1]
            n = hm // dil
            src = slice(r0 // dil, r0 // dil + n)
            base = bi * 2 * n_grp
            for r in range(dil):
                rows = pl.ds(r0 + r, n, stride=dil)
                for j in range(n_grp):
                    sl = slice(j * LANES, (j + 1) * LANES)
                    slab_ref[base + j, rows, :] = o_ref[0, r, src, sl].astype(F32)
                    slab_ref[base + n_grp + j, rows, :] = l_ref[0, r, src, sl]
        yield

        pieces = []
        ssq = jnp.zeros((hm, 1), F32)
        for hp in range(n_grp):
            sl = slice(hp * LANES, (hp + 1) * LANES)
            e1 = l1_ref[0, 0, rs, sl]
            e2 = slab_ref[n_grp + hp, rs, :]
            e3 = slab_ref[3 * n_grp + hp, rs, :]
            mx = jnp.maximum(jnp.maximum(e1, e2), e3)
            w1, w2, w3 = jnp.exp(e1 - mx), jnp.exp(e2 - mx), jnp.exp(e3 - mx)
            num = (w1 * o1_ref[0, 0, rs, sl].astype(F32) + w2 * slab_ref[hp, rs, :]
                   + w3 * slab_ref[2 * n_grp + hp, rs, :])
            a = num / (w1 + w2 + w3)
            pieces.append(a)
            ssq = ssq + jnp.sum(a * a, axis=1, keepdims=True)
        inv = lax.rsqrt(ssq * (1.0 / ATTN_WIDTH) + NORM_EPS)
        mixed_ref[:, 0:CONV_CH] = mc_ref[rs, :]
        for hp, a in enumerate(pieces):
            sl = slice(hp * LANES, (hp + 1) * LANES)
            mixed_ref[:, CONV_CH + hp * LANES:CONV_CH + (hp + 1) * LANES] = (
                a * inv * ag_ref[:, sl]).astype(BF16)
        yield
        x1 = x_ref[rs, :] + _dot(mixed_ref[...], wout_ref[...])
        yield

        hd = d // MEM_HEADS
        qx = (_dot(_rms(x1, xg_ref[...]).astype(BF16), wq_ref[...]) * (hd ** -0.5)).astype(BF16)
        yield
        for h in range(MEM_HEADS):
            sl = slice(h * hd, (h + 1) * hd)
            s = _dot_nt(qx[:, sl], kx_ref[0, :, sl])
            m = jnp.max(s, axis=1, keepdims=True)
            p = jnp.exp(s - m)
            l = jnp.sum(p, axis=1, keepdims=True)
            ox_ref[:, sl] = (_dot(p.astype(BF16), vx_ref[0, :, sl]) / l).astype(BF16)
        yield
        x2 = x1 + _dot(ox_ref[...], wo_ref[...])
        x2_ref[rs, :] = x2
        yield

        h3 = _rms(x2, mg_ref[...])
        h3_ref[rs, :] = h3.astype(BF16)
        h_hi = h3.astype(BF16)
        h_lo = (h3 - h_hi.astype(F32)).astype(BF16)
        lg = _dot(h_hi, w_hi) + (_dot(h_hi, w_lo) + _dot(h_lo, w_hi)) + br_ref[...]

        is_group = lane < N_GROUPS
        gmax, gidx = first_max(jnp.where(is_group, lg, NEG))
        p_g = 1.0 / jnp.sum(jnp.where(is_group, jnp.exp(lg - gmax), 0.0), axis=1, keepdims=True)
        lo = N_GROUPS + EXPERTS_PER_GROUP * gidx
        el = jnp.where((lane >= lo) & (lane < lo + EXPERTS_PER_GROUP), lg, NEG)
        v1, i1 = first_max(el)
        v2, i2 = first_max(jnp.where(lane == i1, NEG, el))
        t2 = jnp.exp(v2 - v1)
        g1 = p_g / (1.0 + t2)
        g2 = g1 * t2
        e1 = i1 - N_GROUPS
        e2 = i2 - N_GROUPS

        hits = jnp.where((lane == e1) | (lane == e2), 1.0, 0.0)
        tot_box[0] = tot_box[0] + jnp.sum(hits, axis=0, keepdims=True)

        route = jnp.zeros((hm, LANES), F32)
        for pos, val in enumerate((e1.astype(F32), e2.astype(F32), g1, g2)):
            route = jnp.where(lane == pos, val, route)
        route_ref[rs, :] = route
        yield

    chains = [chain(i * hm, scratch[2 * i], scratch[2 * i + 1]) for i in range(tm // hm)]
    n_stage = 7
    for step in range(n_stage + len(chains) - 1):
        for ci, ch in enumerate(chains):
            if 0 <= step - ci < n_stage:
                next(ch)
    tot_ref[0] = tot_box[0]


def _post_mix(o_list, lse_list, mixc, xf, ag, w_out, xg, w_xq, kx, vx, w_xo, mg, w_rt, b_rt, seq):
    t, d = xf.shape
    tm = TM_POST
    blk_per_seq = seq // tm
    row = lambda i: (i, 0)
    const = lambda i: (0, 0)
    bmap = lambda i: (i // blk_per_seq, 0, 0)
    mlen = kx.shape[1]
    cls = lambda i: (i // blk_per_seq, 0, i % blk_per_seq, 0)
    wide = [pl.BlockSpec((1, dil, tm // dil, ATTN_WIDTH), cls) for dil in DILATIONS]
    narrow = pl.BlockSpec((tm, LANES), row)
    full = pl.BlockSpec((tm, d), row)
    mat = pl.BlockSpec((d, d), const)
    n_slab = 4 * (ATTN_WIDTH // LANES)
    return pl.pallas_call(
        _post_kernel,
        grid=(t // tm,),
        in_specs=[*wide, *wide,
                  pl.BlockSpec((tm, CONV_CH), row), full,
                  pl.BlockSpec((1, ATTN_WIDTH), const), mat,
                  pl.BlockSpec((1, d), const), mat,
                  pl.BlockSpec((1, mlen, d), bmap), pl.BlockSpec((1, mlen, d), bmap), mat,
                  pl.BlockSpec((1, d), const), pl.BlockSpec((d, LANES), const),
                  pl.BlockSpec((1, LANES), const)],
        out_specs=[full, full, narrow, pl.BlockSpec((1, 1, LANES), lambda i: (i, 0, 0))],
        out_shape=[jax.ShapeDtypeStruct((t, d), F32), jax.ShapeDtypeStruct((t, d), BF16),
                   jax.ShapeDtypeStruct((t, LANES), F32),
                   jax.ShapeDtypeStruct((t // tm, 1, LANES), F32)],
        scratch_shapes=[pltpu.VMEM((n_slab, tm, LANES), F32)]
        + [pltpu.VMEM((POST_HALF, d), BF16)] * (2 * (tm // POST_HALF)),
        compiler_params=pltpu.CompilerParams(
            dimension_semantics=("parallel",), vmem_limit_bytes=VMEM_LIMIT),
        name="post_mix",
    )(*o_list, *lse_list, mixc, xf, ag, w_out, xg, w_xq, kx, vx, w_xo, mg, w_rt, b_rt)


ROW_ALIGN = 8
RUN_PIECES = tuple(2 ** k for k in range(9, 2, -1))
TILE_PIECES = (1024,) + RUN_PIECES
LOCAL_ROWS = 2 * TM_ROW + ROW_ALIGN * N_EXPERTS


def _run_copies(tab_ref, local_ref, global_ref, sem, to_global):
    def body(e, carry):
        ls = tab_ref[0, 0, e]
        gs = tab_ref[0, 0, N_EXPERTS + e]
        n = tab_ref[0, 0, 2 * N_EXPERTS + e]
        for size in RUN_PIECES:
            done = (n // (2 * size)) * (2 * size)

            @pl.when((n & size) != 0)
            def _():
                loc = local_ref.at[pl.ds(pl.multiple_of(ls + done, ROW_ALIGN), size)]
                glo = global_ref.at[pl.ds(pl.multiple_of(gs + done, ROW_ALIGN), size)]
                src, dst = (loc, glo) if to_global else (glo, loc)
                pltpu.make_async_copy(src, dst, sem).start()
        return carry

    lax.fori_loop(0, N_EXPERTS, body, 0)


def _run_waits(total, local_ref, global_ref, sem):
    for size in TILE_PIECES:
        @pl.when((total & size) != 0)
        def _():
            pltpu.make_async_copy(local_ref.at[pl.ds(0, size)], global_ref.at[pl.ds(0, size)],
                                  sem).wait()


def _local_positions(route, lstart):
    tm = route.shape[0]
    lane = lax.broadcasted_iota(jnp.int32, (tm, LANES), 1)
    rr = lax.broadcasted_iota(jnp.int32, (tm, tm), 0)
    cc = lax.broadcasted_iota(jnp.int32, (tm, tm), 1)
    ltri = jnp.where(cc < rr, 1.0, 0.0).astype(BF16)
    oh1 = jnp.where(lane == route[:, 0:1].astype(jnp.int32), 1.0, 0.0)
    oh2 = jnp.where(lane == route[:, 1:2].astype(jnp.int32), 1.0, 0.0)
    pre1 = _dot(ltri, oh1.astype(BF16))
    pre2 = _dot(ltri, oh2.astype(BF16))
    tot1 = jnp.sum(oh1, axis=0, keepdims=True)
    lp1 = jnp.sum(oh1 * (lstart + pre1), axis=1, keepdims=True)
    lp2 = jnp.sum(oh2 * (lstart + tot1 + pre2), axis=1, keepdims=True)
    return lp1, lp2


def _dispatch_kernel(ps_ref, pe_ref, tab_ref, ls_ref, route_ref, h_ref, xout_ref, lp_ref,
                     zero_ref, xs_ref, pend_ref, sems):
    tm = h_ref.shape[0]
    bm = zero_ref.shape[0]
    sem = sems.at[2]

    @pl.when(pl.program_id(0) == 0)
    def _():
        zero_ref[...] = jnp.zeros_like(zero_ref)

        def pad_copy(e):
            last = pl.multiple_of(pe_ref[e] - bm, bm)
            return pltpu.make_async_copy(zero_ref, xout_ref.at[pl.ds(last, bm)], sem)

        def nonempty(e):
            return pe_ref[e] > ps_ref[e]

        def start(e, carry):
            pl.when(nonempty(e))(lambda: pad_copy(e).start())
            return carry

        def wait(e, carry):
            pl.when(nonempty(e))(lambda: pad_copy(e).wait())
            return carry

        lax.fori_loop(0, N_EXPERTS, start, 0)
        lax.fori_loop(0, N_EXPERTS, wait, 0)

        def tail_copy(b):
            return pltpu.make_async_copy(
                zero_ref, xout_ref.at[pl.ds(pl.multiple_of(b * bm, bm), bm)], sem)

        def tail_start(b, carry):
            tail_copy(b).start()
            return carry

        def tail_wait(b, carry):
            tail_copy(b).wait()
            return carry

        first_tail = pe_ref[N_EXPERTS - 1] // bm
        n_blk = xout_ref.shape[0] // bm
        lax.fori_loop(first_tail, n_blk, tail_start, 0)
        lax.fori_loop(first_tail, n_blk, tail_wait, 0)

    i = pl.program_id(0)
    n_step = pl.num_programs(0)
    slot = i % 2

    def drain(s):
        _run_waits(pend_ref[s], xs_ref.at[s], xout_ref, sems.at[s])

    @pl.when(i >= 2)
    def _():
        drain(slot)

    lp1, lp2 = _local_positions(route_ref[...], ls_ref[0])
    lane = lax.broadcasted_iota(jnp.int32, (tm, LANES), 1)
    lp_ref[...] = jnp.where(lane == 0, lp1, jnp.where(lane == 1, lp2, 0.0))
    row1 = jnp.broadcast_to(lp1, (tm, LANES)).T[0:1, :].astype(jnp.int32)
    row2 = jnp.broadcast_to(lp2, (tm, LANES)).T[0:1, :].astype(jnp.int32)
    slot_id = lax.broadcasted_iota(jnp.int32, (LOCAL_ROWS, tm), 0)
    perm = jnp.where((slot_id == row1) | (slot_id == row2), 1.0, 0.0).astype(BF16)
    xs_ref[slot] = _dot(perm, h_ref[...])
    _run_copies(tab_ref, xs_ref.at[slot], xout_ref, sems.at[slot], to_global=True)
    pend_ref[slot] = tab_ref[0, 0, 3 * N_EXPERTS]

    @pl.when(i == n_step - 1)
    def _():
        drain(slot)

        @pl.when(n_step > 1)
        def _():
            drain(1 - slot)


def _dispatch(pstarts, pends, tab, lstart, route, h3, n_rows):
    t, d = h3.shape
    tm = TM_ROW
    tile = lambda i, ps, pe: (i, 0, 0)
    row = lambda i, ps, pe: (i, 0)
    return pl.pallas_call(
        _dispatch_kernel,
        grid_spec=pltpu.PrefetchScalarGridSpec(
            num_scalar_prefetch=2,
            grid=(t // tm,),
            in_specs=[pl.BlockSpec((1, 1, LANES), tile, memory_space=pltpu.SMEM),
                      pl.BlockSpec((1, 1, LANES), tile),
                      pl.BlockSpec((tm, LANES), row),
                      pl.BlockSpec((tm, d), row)],
            out_specs=[pl.BlockSpec(memory_space=pl.ANY), pl.BlockSpec((tm, LANES), row)],
            scratch_shapes=[pltpu.VMEM((BM_EXPERT, d), F32), pltpu.VMEM((2, LOCAL_ROWS, d), F32),
                            pltpu.SMEM((2,), jnp.int32), pltpu.SemaphoreType.DMA((3,))]),
        out_shape=[jax.ShapeDtypeStruct((n_rows, d), F32),
                   jax.ShapeDtypeStruct((t, LANES), F32)],
        compiler_params=pltpu.CompilerParams(
            dimension_semantics=("arbitrary",), vmem_limit_bytes=VMEM_LIMIT),
        name="moe_dispatch",
    )(pstarts, pends, tab, lstart, route, h3)


def _expert_kernel(be_ref, nu_ref, x_ref, w1_ref, w3_ref, w2_ref, y_ref, w1b, w3b, w2b):
    i = pl.program_id(0)
    prev = be_ref[jnp.maximum(i - 1, 0)]

    @pl.when((i == 0) | (be_ref[i] != prev))
    def _():
        w1b[...] = w1_ref[0].astype(BF16)
        w3b[...] = w3_ref[0].astype(BF16)
        w2b[...] = w2_ref[0].astype(BF16)

    @pl.when(i < nu_ref[0])
    def _():
        xb = x_ref[...].astype(BF16)
        a = _dot(xb, w1b[...])
        g = _dot(xb, w3b[...])
        act = (a / (1.0 + jnp.exp(-a)) * g).astype(BF16)
        y_ref[...] = _dot(act, w2b[...])

    @pl.when(i >= nu_ref[0])
    def _():
        y_ref[...] = jnp.zeros_like(y_ref)


def _experts(blk_e, n_used, x_rows, w1, w3, w2):
    p, d = x_rows.shape
    ff = w1.shape[2]
    bm = BM_EXPERT
    wmap = lambda i, be, nu: (be[i], 0, 0)
    return pl.pallas_call(
        _expert_kernel,
        grid_spec=pltpu.PrefetchScalarGridSpec(
            num_scalar_prefetch=2,
            grid=(p // bm,),
            in_specs=[pl.BlockSpec((bm, d), lambda i, be, nu: (jnp.minimum(i, nu[0] - 1), 0)),
                      pl.BlockSpec((1, d, ff), wmap),
                      pl.BlockSpec((1, d, ff), wmap),
                      pl.BlockSpec((1, ff, d), wmap)],
            out_specs=pl.BlockSpec((bm, d), lambda i, be, nu: (i, 0)),
            scratch_shapes=[pltpu.VMEM((d, ff), BF16), pltpu.VMEM((d, ff), BF16),
                            pltpu.VMEM((ff, d), BF16)]),
        out_shape=jax.ShapeDtypeStruct((p, d), F32),
        compiler_params=pltpu.CompilerParams(
            dimension_semantics=("arbitrary",), vmem_limit_bytes=VMEM_LIMIT),
        name="moe_experts",
    )(blk_e, n_used, x_rows, w1, w3, w2)


def _combine_kernel(tab_ref, nxt_ref, x2_ref, route_ref, lp_ref, g_ref, y_ref, o_ref, ys_ref, sems):
    tm = x2_ref.shape[0]
    i = pl.program_id(0)
    n_step = pl.num_programs(0)
    slot = i % 2

    @pl.when(i == 0)
    def _():
        ys_ref[...] = jnp.zeros_like(ys_ref)
        _run_copies(tab_ref, ys_ref.at[0], y_ref, sems.at[0], to_global=False)

    @pl.when(i + 1 < n_step)
    def _():
        _run_copies(nxt_ref, ys_ref.at[1 - slot], y_ref, sems.at[1 - slot], to_global=False)

    _run_waits(tab_ref[0, 0, 3 * N_EXPERTS], ys_ref.at[slot], y_ref, sems.at[slot])

    route = route_ref[...]
    lp = lp_ref[...]
    slot_id = lax.broadcasted_iota(jnp.int32, (tm, LOCAL_ROWS), 1)
    sel = (jnp.where(slot_id == lp[:, 0:1].astype(jnp.int32), route[:, 2:3], 0.0)
           + jnp.where(slot_id == lp[:, 1:2].astype(jnp.int32), route[:, 3:4], 0.0))
    x3 = x2_ref[...] + _dot(sel.astype(BF16), ys_ref[slot].astype(BF16))
    o_ref[...] = _rms(x3, g_ref[...])


def _combine(tab, x2, route, lp, g, y_rows):
    t, d = x2.shape
    tm = TM_ROW
    n_tile = t // tm
    row = lambda i: (i, 0)
    return pl.pallas_call(
        _combine_kernel,
        grid=(n_tile,),
        in_specs=[pl.BlockSpec((1, 1, LANES), lambda i: (i, 0, 0), memory_space=pltpu.SMEM),
                  pl.BlockSpec((1, 1, LANES), lambda i: (jnp.minimum(i + 1, n_tile - 1), 0, 0),
                               memory_space=pltpu.SMEM),
                  pl.BlockSpec((tm, d), row),
                  pl.BlockSpec((tm, LANES), row),
                  pl.BlockSpec((tm, LANES), row),
                  pl.BlockSpec((1, d), lambda i: (0, 0)),
                  pl.BlockSpec(memory_space=pl.ANY)],
        out_specs=pl.BlockSpec((tm, d), row),
        out_shape=jax.ShapeDtypeStruct((t, d), F32),
        scratch_shapes=[pltpu.VMEM((2, LOCAL_ROWS, d), F32), pltpu.SemaphoreType.DMA((2,))],
        compiler_params=pltpu.CompilerParams(
            dimension_semantics=("arbitrary",), vmem_limit_bytes=VMEM_LIMIT),
        name="moe_combine",
    )(tab, tab, x2, route, lp, g, y_rows)


def kernel(x, mem, positions, mix_norm_g, w_in, conv_dw_w, conv_dw_b, conv_ln_g, conv_ln_b,
           conv_out_g, attn_out_g, w_out, xattn_norm_g, mem_norm_g, w_xq, w_xk, w_xv, w_xo,
           moe_norm_g, w_group, b_group, w_router, b_router, w1, w3, w2, final_norm_g):
    bsz, seq, d = x.shape
    assert w_in.shape[0] == 1, "single-layer encoder only"
    l = 0
    t = bsz * seq
    vec = lambda a: a.reshape(1, -1)
    cos_t, sin_t = _rope_tables(positions)
    xf = x.reshape(t, d)
    c, *qkv = _in_projection(xf, vec(mix_norm_g[l]), w_in[l].astype(BF16), cos_t, sin_t, bsz, seq)
    mixc = _conformer_conv(c.reshape(bsz, seq, CONV_CH), conv_dw_w[l], vec(conv_dw_b[l]),
                           vec(conv_ln_g[l]), vec(conv_ln_b[l]), vec(conv_out_g[l]))
    branches = [_band_attention(*qkv[3 * i:3 * i + 3]) for i in range(len(DILATIONS))]
    kx, vx = _mem_kv(mem, vec(mem_norm_g[l]), w_xk[l].astype(BF16), w_xv[l].astype(BF16))
    pad = LANES - N_GROUPS - N_EXPERTS
    w_rt = jnp.pad(jnp.concatenate([w_group[l], w_router[l]], axis=1), ((0, 0), (0, pad)))
    b_rt = jnp.pad(jnp.concatenate([b_group[l], b_router[l]]), (0, pad)).reshape(1, LANES)
    x2, h3, route, tile_tot = _post_mix(
        [o for o, _ in branches], [s for _, s in branches], mixc.reshape(t, CONV_CH), xf,
        vec(attn_out_g[l]), w_out[l].astype(BF16), vec(xattn_norm_g[l]), w_xq[l].astype(BF16),
        kx, vx, w_xo[l].astype(BF16), vec(moe_norm_g[l]), w_rt, b_rt, seq)

    assert TM_ROW == TM_POST
    bm = BM_EXPERT
    n_tile = t // TM_ROW
    tt = tile_tot[:, 0, :N_EXPERTS].astype(jnp.int32)
    tt = (tt + ROW_ALIGN - 1) // ROW_ALIGN * ROW_ALIGN
    lstart = jnp.cumsum(tt, axis=1) - tt
    before = jnp.cumsum(tt, axis=0) - tt
    pcounts = (jnp.sum(tt, axis=0) + bm - 1) // bm * bm
    pends = jnp.cumsum(pcounts).astype(jnp.int32)
    pstarts = pends - pcounts
    n_blk = (2 * t + (ROW_ALIGN - 1) * n_tile * N_EXPERTS) // bm + N_EXPERTS
    n_used = pends[-1:] // bm
    blk = jnp.minimum(jnp.arange(n_blk, dtype=jnp.int32), n_used - 1)
    blk_e = jnp.sum((blk[:, None] * bm >= pends[None, :]).astype(jnp.int32), axis=1)
    tile_rows = jnp.sum(tt, axis=1, keepdims=True)
    tab = jnp.concatenate([lstart, pstarts[None, :] + before, tt, tile_rows,
                           jnp.zeros((n_tile, N_EXPERTS - 1), jnp.int32)], axis=1)[:, None, :]
    lstart_v = jnp.pad(lstart.astype(F32), ((0, 0), (0, LANES - N_EXPERTS)))[:, None, :]

    x_rows, lp = _dispatch(pstarts, pends, tab, lstart_v, route, h3, n_blk * bm)
    y_rows = _experts(blk_e, n_used, x_rows, w1[l], w3[l], w2[l])
    out = _combine(tab, x2, route, lp, vec(final_norm_g), y_rows)
    return out.reshape(bsz, seq, d)
```

```python
import functools

import jax
import jax.numpy as jnp
from jax import lax
from jax.experimental import pallas as pl
from jax.experimental.pallas import tpu as pltpu

HEAD_DIM = 64
CONV_CH = 256
ATTN_HEADS = 12
ATTN_WIDTH = ATTN_HEADS * HEAD_DIM
CONV_KERNEL = 31
CONV_PAD = 16
BAND = 64
DILATIONS = (1, 4, 16)
ROPE_THETA = 10000.0
MEM_HEADS = 4
N_GROUPS = 4
EXPERTS_PER_GROUP = 8
N_EXPERTS = N_GROUPS * EXPERTS_PER_GROUP
NORM_EPS = 1e-6
LN_EPS = 1e-5
LANES = 128
NEG = -1e30
LOG2_E = 1.4426950408889634
LN_2 = 0.6931471805599453

TM_IN = 512
TM_POST = 512
POST_HALF = 512
TM_ROW = 512
BM_EXPERT = 512
CONV_CHUNK = 128
ATTN_SUB = 128
ATTN_Q_BLOCK = 512
VMEM_LIMIT = 56 * 1024 * 1024

F32 = jnp.float32
BF16 = jnp.bfloat16


def _rms(x, g):
    return x * lax.rsqrt(jnp.mean(x * x, axis=-1, keepdims=True) + NORM_EPS) * g


def _dot(a, b):
    return jnp.dot(a, b, preferred_element_type=F32)


def _dot_nt(a, b):
    return lax.dot_general(a, b, (((1,), (1,)), ((), ())), preferred_element_type=F32)


def _rope_kernel(pos_ref, inv_ref, sign_ref, cos_ref, sin_ref):
    ang = pos_ref[...].astype(F32) * inv_ref[...]
    cos_ref[...] = jnp.cos(ang)
    sin_ref[...] = jnp.sin(ang) * sign_ref[...]


def _rope_tables(positions):
    s = positions.shape[0]
    half = HEAD_DIM // 2
    inv = 1.0 / (ROPE_THETA ** (jnp.arange(half, dtype=F32) * (2.0 / HEAD_DIM)))
    inv = jnp.tile(inv, LANES // half)[None, :]
    sign = jnp.tile(jnp.concatenate([-jnp.ones((half,), F32), jnp.ones((half,), F32)]),
                    LANES // HEAD_DIM)[None, :]
    return pl.pallas_call(
        _rope_kernel,
        out_shape=(jax.ShapeDtypeStruct((s, LANES), F32), jax.ShapeDtypeStruct((s, LANES), F32)),
        name="rope_tables",
    )(positions.reshape(s, 1), inv, sign)


def _inproj_kernel(x_ref, g_ref, w_ref, cos_ref, sin_ref, c_ref, *rest):
    out_refs, slab_ref = rest[:-1], rest[-1]
    tm = x_ref.shape[0]
    n_grp = ATTN_WIDTH // LANES
    h = _rms(x_ref[...], g_ref[...]).astype(BF16)
    u = _dot(h, w_ref[:, 0:2 * CONV_CH])
    c_ref[...] = u[:, :CONV_CH] / (1.0 + jnp.exp(-u[:, CONV_CH:]))
    cos = cos_ref[...]
    sin = sin_ref[...]
    lane = lax.broadcasted_iota(jnp.int32, cos.shape, 1)
    first_half = (lane % HEAD_DIM) < (HEAD_DIM // 2)
    off = 2 * CONV_CH
    for which, scale in enumerate((HEAD_DIM ** -0.5 * LOG2_E, 1.0, None)):
        u = _dot(h, w_ref[:, off:off + ATTN_WIDTH])
        off += ATTN_WIDTH
        for j in range(n_grp):
            xs = u[:, j * LANES:(j + 1) * LANES]
            if scale is not None:
                partner = jnp.where(first_half, pltpu.roll(xs, LANES - 32, 1),
                                    pltpu.roll(xs, 32, 1))
                xs = (xs * cos + partner * sin) * scale
            slab_ref[which * n_grp + j] = xs
        for di, dil in enumerate(DILATIONS):
            ref = out_refs[3 * di + which]
            n = tm // dil
            for r in range(dil):
                for j in range(n_grp):
                    idx = which * n_grp + j
                    rows = slab_ref[idx] if dil == 1 else slab_ref[idx, pl.ds(r, n, stride=dil), :]
                    ref[0, r, :, j * LANES:(j + 1) * LANES] = rows.astype(BF16)


def _in_projection(xf, g, w_in, cos_t, sin_t, bsz, seq):
    t, d = xf.shape
    tm = TM_IN
    n_pos_blk = seq // tm
    row = lambda i: (i, 0)
    const = lambda i: (0, 0)
    cls = lambda i: (i // n_pos_blk, 0, i % n_pos_blk, 0)
    out_specs = [pl.BlockSpec((tm, CONV_CH), row)]
    out_shape = [jax.ShapeDtypeStruct((t, CONV_CH), F32)]
    for dil in DILATIONS:
        for _ in range(3):
            out_specs.append(pl.BlockSpec((1, dil, tm // dil, ATTN_WIDTH), cls))
            out_shape.append(jax.ShapeDtypeStruct((bsz, dil, seq // dil, ATTN_WIDTH), BF16))
    return pl.pallas_call(
        _inproj_kernel,
        grid=(t // tm,),
        in_specs=[
            pl.BlockSpec((tm, d), row),
            pl.BlockSpec((1, d), const),
            pl.BlockSpec(w_in.shape, const),
            pl.BlockSpec((tm, LANES), lambda i: (i % n_pos_blk, 0)),
            pl.BlockSpec((tm, LANES), lambda i: (i % n_pos_blk, 0)),
        ],
        out_specs=out_specs,
        out_shape=out_shape,
        scratch_shapes=[pltpu.VMEM((3 * ATTN_WIDTH // LANES, tm, LANES), F32)],
        compiler_params=pltpu.CompilerParams(
            dimension_semantics=("parallel",), vmem_limit_bytes=VMEM_LIMIT),
        name="in_projection",
    )(xf, g, w_in, cos_t, sin_t)


def _conv_kernel(c_ref, w_ref, b_ref, lng_ref, lnb_ref, og_ref, o_ref, pad_ref):
    seq = c_ref.shape[1]
    n_slab = CONV_CH // LANES
    zeros = jnp.zeros((CONV_PAD, LANES), F32)
    for h in range(n_slab):
        pad_ref[h, 0:CONV_PAD, :] = zeros
        pad_ref[h, seq + CONV_PAD:seq + 2 * CONV_PAD, :] = zeros
        pad_ref[h, CONV_PAD:seq + CONV_PAD, :] = c_ref[0, :, h * LANES:(h + 1) * LANES]
    shift = CONV_PAD - CONV_KERNEL // 2

    def body(i, carry):
        base = pl.multiple_of(i * CONV_CHUNK, CONV_CHUNK)
        acc = []
        for h in range(n_slab):
            sl = slice(h * LANES, (h + 1) * LANES)
            a = jnp.zeros((CONV_CHUNK, LANES), F32)
            for k in range(CONV_KERNEL):
                a = a + pad_ref[h, pl.ds(base + (k + shift), CONV_CHUNK), :] * w_ref[k:k + 1, sl]
            acc.append(a + b_ref[:, sl])
        inv_ch = 1.0 / CONV_CH
        mu = sum(jnp.sum(a, axis=-1, keepdims=True) for a in acc) * inv_ch
        cen = [a - mu for a in acc]
        var = sum(jnp.sum(c * c, axis=-1, keepdims=True) for c in cen) * inv_ch
        rstd = lax.rsqrt(var + LN_EPS)
        ys = []
        for h, c in enumerate(cen):
            sl = slice(h * LANES, (h + 1) * LANES)
            y = c * rstd * lng_ref[:, sl] + lnb_ref[:, sl]
            ys.append(y / (1.0 + jnp.exp(-y)))
        ms = sum(jnp.sum(y * y, axis=-1, keepdims=True) for y in ys) * inv_ch
        rinv = lax.rsqrt(ms + NORM_EPS)
        for h, y in enumerate(ys):
            sl = slice(h * LANES, (h + 1) * LANES)
            o_ref[0, pl.ds(base, CONV_CHUNK), sl] = (y * rinv * og_ref[:, sl]).astype(BF16)
        return carry

    lax.fori_loop(0, seq // CONV_CHUNK, body, 0, unroll=2)


def _conformer_conv(c, w, b, lng, lnb, og):
    bsz, seq, ch = c.shape
    vec = pl.BlockSpec((1, ch), lambda i: (0, 0))
    return pl.pallas_call(
        _conv_kernel,
        grid=(bsz,),
        in_specs=[pl.BlockSpec((1, seq, ch), lambda i: (i, 0, 0)),
                  pl.BlockSpec(w.shape, lambda i: (0, 0)), vec, vec, vec, vec],
        out_specs=pl.BlockSpec((1, seq, ch), lambda i: (i, 0, 0)),
        out_shape=jax.ShapeDtypeStruct((bsz, seq, ch), BF16),
        scratch_shapes=[pltpu.VMEM((ch // LANES, seq + 2 * CONV_PAD, LANES), F32)],
        compiler_params=pltpu.CompilerParams(
            dimension_semantics=("parallel",), vmem_limit_bytes=VMEM_LIMIT),
        name="conformer_conv",
    )(c, w, b, lng, lnb, og)


def _band_attn_kernel(q_ref, k_ref, v_ref, o_ref, lse_ref, *, sub_len, q_blk, win):
    sq = ATTN_SUB
    row = lax.broadcasted_iota(jnp.int32, (2 * sq, win), 0) % sq
    col = lax.broadcasted_iota(jnp.int32, (2 * sq, win), 1)
    lane = lax.broadcasted_iota(jnp.int32, (sq, LANES), 1)
    is_a = lane < HEAD_DIM
    def chain(rows, ws, bias, hp):
        sl = slice(hp * LANES, (hp + 1) * LANES)
        qh = q_ref[rows, sl]
        kh = k_ref[pl.ds(ws, win), sl]
        zero = jnp.zeros_like(qh)
        q2 = jnp.concatenate([jnp.where(is_a, qh, zero), jnp.where(is_a, zero, qh)], axis=0)
        s = _dot_nt(q2, kh) + bias
        yield
        m = jnp.max(s, axis=1, keepdims=True)
        p = jnp.exp2(s - m)
        l = jnp.sum(p, axis=1, keepdims=True)
        yield
        o2 = _dot(p.astype(BF16), v_ref[pl.ds(ws, win), sl]) / l
        lse2 = m * LN_2 + jnp.log(l)
        o_ref[rows, sl] = jnp.where(is_a, o2[:sq], o2[sq:]).astype(BF16)
        lse_ref[rows, sl] = jnp.where(is_a, lse2[:sq], lse2[sq:])
        yield

    chains = []
    for sub in range(q_blk // sq):
        q0 = pl.program_id(2) * q_blk + sub * sq
        ws = pl.multiple_of(jnp.clip(q0 - BAND, 0, sub_len - win), BAND)
        bias = jnp.where(jnp.abs(row - col + (q0 - ws)) <= BAND, 0.0, NEG).astype(F32)
        rows = slice(sub * sq, (sub + 1) * sq)
        chains += [chain(rows, ws, bias, hp) for hp in range(ATTN_WIDTH // LANES)]
    n_stage = 3
    for step in range(n_stage + len(chains) - 1):
        for ci in reversed(range(len(chains))):
            if 0 <= step - ci < n_stage:
                next(chains[ci])


def _band_attention(q, k, v):
    bsz, dil, sub_len, width = q.shape
    q_blk = min(ATTN_Q_BLOCK, sub_len)
    win = min(sub_len, ATTN_SUB + 2 * BAND)
    qmap = lambda b, r, i: (b, r, i, 0)
    kmap = lambda b, r, i: (b, r, 0, 0)
    return pl.pallas_call(
        functools.partial(_band_attn_kernel, sub_len=sub_len, q_blk=q_blk, win=win),
        grid=(bsz, dil, sub_len // q_blk),
        in_specs=[pl.BlockSpec((None, None, q_blk, width), qmap),
                  pl.BlockSpec((None, None, sub_len, width), kmap),
                  pl.BlockSpec((None, None, sub_len, width), kmap)],
        out_specs=[pl.BlockSpec((None, None, q_blk, width), qmap),
                   pl.BlockSpec((None, None, q_blk, width), qmap)],
        out_shape=[jax.ShapeDtypeStruct((bsz, dil, sub_len, width), BF16),
                   jax.ShapeDtypeStruct((bsz, dil, sub_len, width), F32)],
        compiler_params=pltpu.CompilerParams(
            dimension_semantics=("parallel", "parallel", "arbitrary"),
            vmem_limit_bytes=VMEM_LIMIT),
        name=f"band_attention_d{dil}",
    )(q, k, v)


def _memkv_kernel(m_ref, g_ref, wk_ref, wv_ref, k_ref, v_ref):
    h = _rms(m_ref[0], g_ref[...]).astype(BF16)
    k_ref[0] = _dot(h, wk_ref[...]).astype(BF16)
    v_ref[0] = _dot(h, wv_ref[...]).astype(BF16)


def _mem_kv(mem, g, wk, wv):
    bsz, m, d = mem.shape
    blk = pl.BlockSpec((1, m, d), lambda i: (i, 0, 0))
    const = lambda i: (0, 0)
    return pl.pallas_call(
        _memkv_kernel,
        grid=(bsz,),
        in_specs=[blk, pl.BlockSpec((1, d), const), pl.BlockSpec((d, d), const),
                  pl.BlockSpec((d, d), const)],
        out_specs=[blk, blk],
        out_shape=[jax.ShapeDtypeStruct((bsz, m, d), BF16)] * 2,
        compiler_params=pltpu.CompilerParams(dimension_semantics=("parallel",)),
        name="mem_kv",
    )(mem, g, wk, wv)


def _post_kernel(o1_ref, o2_ref, o3_ref, l1_ref, l2_ref, l3_ref, mc_ref, x_ref,
                 ag_ref, wout_ref, xg_ref, wq_ref, kx_ref, vx_ref, wo_ref,
                 mg_ref, wr_ref, br_ref,
                 x2_ref, h3_ref, route_ref, tot_ref,
                 slab_ref, *scratch):
    tm = x_ref.shape[0]
    d = x_ref.shape[1]
    n_grp = ATTN_WIDTH // LANES

    hm = POST_HALF
    lane = lax.broadcasted_iota(jnp.int32, (hm, LANES), 1)
    is_a = lane < HEAD_DIM
    w_rt = wr_ref[...]
    w_hi = w_rt.astype(BF16)
    w_lo = (w_rt - w_hi.astype(F32)).astype(BF16)

    def first_max(vals):
        vmax = jnp.max(vals, axis=1, keepdims=True)
        idx = jnp.min(jnp.where(vals == vmax, lane, LANES), axis=1, keepdims=True)
        return vmax, idx

    tot_box = [jnp.zeros((1, LANES), F32)]

    def chain(r0, mixed_ref, ox_ref):
        rs = slice(r0, r0 + hm)

        for bi, (o_ref, l_ref) in enumerate(((o2_ref, l2_ref), (o3_ref, l3_ref))):
            dil = o_ref.shape[1]
            n = hm // dil
            src = slice(r0 // dil, r0 // dil + n)
            base = bi * 2 * n_grp
            for r in range(dil):
                rows = pl.ds(r0 + r, n, stride=dil)
                for j in range(n_grp):
                    sl = slice(j * LANES, (j + 1) * LANES)
                    slab_ref[base + j, rows, :] = o_ref[0, r, src, sl].astype(F32)
                    slab_ref[base + n_grp + j, rows, :] = l_ref[0, r, src, sl]
        yield

        pieces = []
        ssq = jnp.zeros((hm, 1), F32)
        for hp in range(n_grp):
            sl = slice(hp * LANES, (hp + 1) * LANES)
            e1 = l1_ref[0, 0, rs, sl]
            e2 = slab_ref[n_grp + hp, rs, :]
            e3 = slab_ref[3 * n_grp + hp, rs, :]
            mx = jnp.maximum(jnp.maximum(e1, e2), e3)
            w1, w2, w3 = jnp.exp(e1 - mx), jnp.exp(e2 - mx), jnp.exp(e3 - mx)
            num = (w1 * o1_ref[0, 0, rs, sl].astype(F32) + w2 * slab_ref[hp, rs, :]
                   + w3 * slab_ref[2 * n_grp + hp, rs, :])
            a = num / (w1 + w2 + w3)
            pieces.append(a)
            ssq = ssq + jnp.sum(a * a, axis=1, keepdims=True)
        inv = lax.rsqrt(ssq * (1.0 / ATTN_WIDTH) + NORM_EPS)
        mixed_ref[:, 0:CONV_CH] = mc_ref[rs, :]
        for hp, a in enumerate(pieces):
            sl = slice(hp * LANES, (hp + 1) * LANES)
            mixed_ref[:, CONV_CH + hp * LANES:CONV_CH + (hp + 1) * LANES] = (
                a * inv * ag_ref[:, sl]).astype(BF16)
        yield
        x1 = x_ref[rs, :] + _dot(mixed_ref[...], wout_ref[...])
        yield

        hd = d // MEM_HEADS
        qx = (_dot(_rms(x1, xg_ref[...]).astype(BF16), wq_ref[...]) * (hd ** -0.5)).astype(BF16)
        yield
        for h in range(MEM_HEADS):
            sl = slice(h * hd, (h + 1) * hd)
            s = _dot_nt(qx[:, sl], kx_ref[0, :, sl])
            m = jnp.max(s, axis=1, keepdims=True)
            p = jnp.exp(s - m)
            l = jnp.sum(p, axis=1, keepdims=True)
            ox_ref[:, sl] = (_dot(p.astype(BF16), vx_ref[0, :, sl]) / l).astype(BF16)
        yield
        x2 = x1 + _dot(ox_ref[...], wo_ref[...])
        x2_ref[rs, :] = x2
        yield

        h3 = _rms(x2, mg_ref[...])
        h3_ref[rs, :] = h3.astype(BF16)
        h_hi = h3.astype(BF16)
        h_lo = (h3 - h_hi.astype(F32)).astype(BF16)
        lg = _dot(h_hi, w_hi) + (_dot(h_hi, w_lo) + _dot(h_lo, w_hi)) + br_ref[...]

        is_group = lane < N_GROUPS
        gmax, gidx = first_max(jnp.where(is_group, lg, NEG))
        p_g = 1.0 / jnp.sum(jnp.where(is_group, jnp.exp(lg - gmax), 0.0), axis=1, keepdims=True)
        lo = N_GROUPS + EXPERTS_PER_GROUP * gidx
        el = jnp.where((lane >= lo) & (lane < lo + EXPERTS_PER_GROUP), lg, NEG)
        v1, i1 = first_max(el)
        v2, i2 = first_max(jnp.where(lane == i1, NEG, el))
        t2 = jnp.exp(v2 - v1)
        g1 = p_g / (1.0 + t2)
        g2 = g1 * t2
        e1 = i1 - N_GROUPS
        e2 = i2 - N_GROUPS

        hits = jnp.where((lane == e1) | (lane == e2), 1.0, 0.0)
        tot_box[0] = tot_box[0] + jnp.sum(hits, axis=0, keepdims=True)

        route = jnp.zeros((hm, LANES), F32)
        for pos, val in enumerate((e1.astype(F32), e2.astype(F32), g1, g2)):
            route = jnp.where(lane == pos, val, route)
        route_ref[rs, :] = route
        yield

    chains = [chain(i * hm, scratch[2 * i], scratch[2 * i + 1]) for i in range(tm // hm)]
    n_stage = 7
    for step in range(n_stage + len(chains) - 1):
        for ci, ch in enumerate(chains):
            if 0 <= step - ci < n_stage:
                next(ch)
    tot_ref[0] = tot_box[0]


def _post_mix(o_list, lse_list, mixc, xf, ag, w_out, xg, w_xq, kx, vx, w_xo, mg, w_rt, b_rt, seq):
    t, d = xf.shape
    tm = TM_POST
    blk_per_seq = seq // tm
    row = lambda i: (i, 0)
    const = lambda i: (0, 0)
    bmap = lambda i: (i // blk_per_seq, 0, 0)
    mlen = kx.shape[1]
    cls = lambda i: (i // blk_per_seq, 0, i % blk_per_seq, 0)
    wide = [pl.BlockSpec((1, dil, tm // dil, ATTN_WIDTH), cls) for dil in DILATIONS]
    narrow = pl.BlockSpec((tm, LANES), row)
    full = pl.BlockSpec((tm, d), row)
    mat = pl.BlockSpec((d, d), const)
    n_slab = 4 * (ATTN_WIDTH // LANES)
    return pl.pallas_call(
        _post_kernel,
        grid=(t // tm,),
        in_specs=[*wide, *wide,
                  pl.BlockSpec((tm, CONV_CH), row), full,
                  pl.BlockSpec((1, ATTN_WIDTH), const), mat,
                  pl.BlockSpec((1, d), const), mat,
                  pl.BlockSpec((1, mlen, d), bmap), pl.BlockSpec((1, mlen, d), bmap), mat,
                  pl.BlockSpec((1, d), const), pl.BlockSpec((d, LANES), const),
                  pl.BlockSpec((1, LANES), const)],
        out_specs=[full, full, narrow, pl.BlockSpec((1, 1, LANES), lambda i: (i, 0, 0))],
        out_shape=[jax.ShapeDtypeStruct((t, d), F32), jax.ShapeDtypeStruct((t, d), BF16),
                   jax.ShapeDtypeStruct((t, LANES), F32),
                   jax.ShapeDtypeStruct((t // tm, 1, LANES), F32)],
        scratch_shapes=[pltpu.VMEM((n_slab, tm, LANES), F32)]
        + [pltpu.VMEM((POST_HALF, d), BF16)] * (2 * (tm // POST_HALF)),
        compiler_params=pltpu.CompilerParams(
            dimension_semantics=("parallel",), vmem_limit_bytes=VMEM_LIMIT),
        name="post_mix",
    )(*o_list, *lse_list, mixc, xf, ag, w_out, xg, w_xq, kx, vx, w_xo, mg, w_rt, b_rt)


ROW_ALIGN = 8
RUN_PIECES = tuple(2 ** k for k in range(9, 2, -1))
TILE_PIECES = (1024,) + RUN_PIECES
LOCAL_ROWS = 2 * TM_ROW + ROW_ALIGN * N_EXPERTS


def _run_copies(tab_ref, local_ref, global_ref, sem, to_global):
    def body(e, carry):
        ls = tab_ref[0, 0, e]
        gs = tab_ref[0, 0, N_EXPERTS + e]
        n = tab_ref[0, 0, 2 * N_EXPERTS + e]
        for size in RUN_PIECES:
            done = (n // (2 * size)) * (2 * size)

            @pl.when((n & size) != 0)
            def _():
                loc = local_ref.at[pl.ds(pl.multiple_of(ls + done, ROW_ALIGN), size)]
                glo = global_ref.at[pl.ds(pl.multiple_of(gs + done, ROW_ALIGN), size)]
                src, dst = (loc, glo) if to_global else (glo, loc)
                pltpu.make_async_copy(src, dst, sem).start()
        return carry

    lax.fori_loop(0, N_EXPERTS, body, 0)


def _run_waits(total, local_ref, global_ref, sem):
    for size in TILE_PIECES:
        @pl.when((total & size) != 0)
        def _():
            pltpu.make_async_copy(local_ref.at[pl.ds(0, size)], global_ref.at[pl.ds(0, size)],
                                  sem).wait()


def _local_positions(route, lstart):
    tm = route.shape[0]
    lane = lax.broadcasted_iota(jnp.int32, (tm, LANES), 1)
    rr = lax.broadcasted_iota(jnp.int32, (tm, tm), 0)
    cc = lax.broadcasted_iota(jnp.int32, (tm, tm), 1)
    ltri = jnp.where(cc < rr, 1.0, 0.0).astype(BF16)
    oh1 = jnp.where(lane == route[:, 0:1].astype(jnp.int32), 1.0, 0.0)
    oh2 = jnp.where(lane == route[:, 1:2].astype(jnp.int32), 1.0, 0.0)
    pre1 = _dot(ltri, oh1.astype(BF16))
    pre2 = _dot(ltri, oh2.astype(BF16))
    tot1 = jnp.sum(oh1, axis=0, keepdims=True)
    lp1 = jnp.sum(oh1 * (lstart + pre1), axis=1, keepdims=True)
    lp2 = jnp.sum(oh2 * (lstart + tot1 + pre2), axis=1, keepdims=True)
    return lp1, lp2


def _dispatch_kernel(ps_ref, pe_ref, tab_ref, ls_ref, route_ref, h_ref, xout_ref, lp_ref,
                     zero_ref, xs_ref, pend_ref, sems):
    tm = h_ref.shape[0]
    bm = zero_ref.shape[0]
    sem = sems.at[2]

    @pl.when(pl.program_id(0) == 0)
    def _():
        zero_ref[...] = jnp.zeros_like(zero_ref)

        def pad_copy(e):
            last = pl.multiple_of(pe_ref[e] - bm, bm)
            return pltpu.make_async_copy(zero_ref, xout_ref.at[pl.ds(last, bm)], sem)

        def nonempty(e):
            return pe_ref[e] > ps_ref[e]

        def start(e, carry):
            pl.when(nonempty(e))(lambda: pad_copy(e).start())
            return carry

        def wait(e, carry):
            pl.when(nonempty(e))(lambda: pad_copy(e).wait())
            return carry

        lax.fori_loop(0, N_EXPERTS, start, 0)
        lax.fori_loop(0, N_EXPERTS, wait, 0)

        def tail_copy(b):
            return pltpu.make_async_copy(
                zero_ref, xout_ref.at[pl.ds(pl.multiple_of(b * bm, bm), bm)], sem)

        def tail_start(b, carry):
            tail_copy(b).start()
            return carry

        def tail_wait(b, carry):
            tail_copy(b).wait()
            return carry

        first_tail = pe_ref[N_EXPERTS - 1] // bm
        n_blk = xout_ref.shape[0] // bm
        lax.fori_loop(first_tail, n_blk, tail_start, 0)
        lax.fori_loop(first_tail, n_blk, tail_wait, 0)

    i = pl.program_id(0)
    n_step = pl.num_programs(0)
    slot = i % 2

    def drain(s):
        _run_waits(pend_ref[s], xs_ref.at[s], xout_ref, sems.at[s])

    @pl.when(i >= 2)
    def _():
        drain(slot)

    lp1, lp2 = _local_positions(route_ref[...], ls_ref[0])
    lane = lax.broadcasted_iota(jnp.int32, (tm, LANES), 1)
    lp_ref[...] = jnp.where(lane == 0, lp1, jnp.where(lane == 1, lp2, 0.0))
    row1 = jnp.broadcast_to(lp1, (tm, LANES)).T[0:1, :].astype(jnp.int32)
    row2 = jnp.broadcast_to(lp2, (tm, LANES)).T[0:1, :].astype(jnp.int32)
    slot_id = lax.broadcasted_iota(jnp.int32, (LOCAL_ROWS, tm), 0)
    perm = jnp.where((slot_id == row1) | (slot_id == row2), 1.0, 0.0).astype(BF16)
    xs_ref[slot] = _dot(perm, h_ref[...])
    _run_copies(tab_ref, xs_ref.at[slot], xout_ref, sems.at[slot], to_global=True)
    pend_ref[slot] = tab_ref[0, 0, 3 * N_EXPERTS]

    @pl.when(i == n_step - 1)
    def _():
        drain(slot)

        @pl.when(n_step > 1)
        def _():
            drain(1 - slot)


def _dispatch(pstarts, pends, tab, lstart, route, h3, n_rows):
    t, d = h3.shape
    tm = TM_ROW
    tile = lambda i, ps, pe: (i, 0, 0)
    row = lambda i, ps, pe: (i, 0)
    return pl.pallas_call(
        _dispatch_kernel,
        grid_spec=pltpu.PrefetchScalarGridSpec(
            num_scalar_prefetch=2,
            grid=(t // tm,),
            in_specs=[pl.BlockSpec((1, 1, LANES), tile, memory_space=pltpu.SMEM),
                      pl.BlockSpec((1, 1, LANES), tile),
                      pl.BlockSpec((tm, LANES), row),
                      pl.BlockSpec((tm, d), row)],
            out_specs=[pl.BlockSpec(memory_space=pl.ANY), pl.BlockSpec((tm, LANES), row)],
            scratch_shapes=[pltpu.VMEM((BM_EXPERT, d), F32), pltpu.VMEM((2, LOCAL_ROWS, d), F32),
                            pltpu.SMEM((2,), jnp.int32), pltpu.SemaphoreType.DMA((3,))]),
        out_shape=[jax.ShapeDtypeStruct((n_rows, d), F32),
                   jax.ShapeDtypeStruct((t, LANES), F32)],
        compiler_params=pltpu.CompilerParams(
            dimension_semantics=("arbitrary",), vmem_limit_bytes=VMEM_LIMIT),
        name="moe_dispatch",
    )(pstarts, pends, tab, lstart, route, h3)


def _expert_kernel(be_ref, nu_ref, x_ref, w1_ref, w3_ref, w2_ref, y_ref, w1b, w3b, w2b):
    i = pl.program_id(0)
    prev = be_ref[jnp.maximum(i - 1, 0)]

    @pl.when((i == 0) | (be_ref[i] != prev))
    def _():
        w1b[...] = w1_ref[0].astype(BF16)
        w3b[...] = w3_ref[0].astype(BF16)
        w2b[...] = w2_ref[0].astype(BF16)

    @pl.when(i < nu_ref[0])
    def _():
        xb = x_ref[...].astype(BF16)
        a = _dot(xb, w1b[...])
        g = _dot(xb, w3b[...])
        act = (a / (1.0 + jnp.exp(-a)) * g).astype(BF16)
        y_ref[...] = _dot(act, w2b[...])

    @pl.when(i >= nu_ref[0])
    def _():
        y_ref[...] = jnp.zeros_like(y_ref)


def _experts(blk_e, n_used, x_rows, w1, w3, w2):
    p, d = x_rows.shape
    ff = w1.shape[2]
    bm = BM_EXPERT
    wmap = lambda i, be, nu: (be[i], 0, 0)
    return pl.pallas_call(
        _expert_kernel,
        grid_spec=pltpu.PrefetchScalarGridSpec(
            num_scalar_prefetch=2,
            grid=(p // bm,),
            in_specs=[pl.BlockSpec((bm, d), lambda i, be, nu: (jnp.minimum(i, nu[0] - 1), 0)),
                      pl.BlockSpec((1, d, ff), wmap),
                      pl.BlockSpec((1, d, ff), wmap),
                      pl.BlockSpec((1, ff, d), wmap)],
            out_specs=pl.BlockSpec((bm, d), lambda i, be, nu: (i, 0)),
            scratch_shapes=[pltpu.VMEM((d, ff), BF16), pltpu.VMEM((d, ff), BF16),
                            pltpu.VMEM((ff, d), BF16)]),
        out_shape=jax.ShapeDtypeStruct((p, d), F32),
        compiler_params=pltpu.CompilerParams(
            dimension_semantics=("arbitrary",), vmem_limit_bytes=VMEM_LIMIT),
        name="moe_experts",
    )(blk_e, n_used, x_rows, w1, w3, w2)


def _combine_kernel(tab_ref, nxt_ref, x2_ref, route_ref, lp_ref, g_ref, y_ref, o_ref, ys_ref, sems):
    tm = x2_ref.shape[0]
    i = pl.program_id(0)
    n_step = pl.num_programs(0)
    slot = i % 2

    @pl.when(i == 0)
    def _():
        ys_ref[...] = jnp.zeros_like(ys_ref)
        _run_copies(tab_ref, ys_ref.at[0], y_ref, sems.at[0], to_global=False)

    @pl.when(i + 1 < n_step)
    def _():
        _run_copies(nxt_ref, ys_ref.at[1 - slot], y_ref, sems.at[1 - slot], to_global=False)

    _run_waits(tab_ref[0, 0, 3 * N_EXPERTS], ys_ref.at[slot], y_ref, sems.at[slot])

    route = route_ref[...]
    lp = lp_ref[...]
    slot_id = lax.broadcasted_iota(jnp.int32, (tm, LOCAL_ROWS), 1)
    sel = (jnp.where(slot_id == lp[:, 0:1].astype(jnp.int32), route[:, 2:3], 0.0)
           + jnp.where(slot_id == lp[:, 1:2].astype(jnp.int32), route[:, 3:4], 0.0))
    x3 = x2_ref[...] + _dot(sel.astype(BF16), ys_ref[slot].astype(BF16))
    o_ref[...] = _rms(x3, g_ref[...])


def _combine(tab, x2, route, lp, g, y_rows):
    t, d = x2.shape
    tm = TM_ROW
    n_tile = t // tm
    row = lambda i: (i, 0)
    return pl.pallas_call(
        _combine_kernel,
        grid=(n_tile,),
        in_specs=[pl.BlockSpec((1, 1, LANES), lambda i: (i, 0, 0), memory_space=pltpu.SMEM),
                  pl.BlockSpec((1, 1, LANES), lambda i: (jnp.minimum(i + 1, n_tile - 1), 0, 0),
                               memory_space=pltpu.SMEM),
                  pl.BlockSpec((tm, d), row),
                  pl.BlockSpec((tm, LANES), row),
                  pl.BlockSpec((tm, LANES), row),
                  pl.BlockSpec((1, d), lambda i: (0, 0)),
                  pl.BlockSpec(memory_space=pl.ANY)],
        out_specs=pl.BlockSpec((tm, d), row),
        out_shape=jax.ShapeDtypeStruct((t, d), F32),
        scratch_shapes=[pltpu.VMEM((2, LOCAL_ROWS, d), F32), pltpu.SemaphoreType.DMA((2,))],
        compiler_params=pltpu.CompilerParams(
            dimension_semantics=("arbitrary",), vmem_limit_bytes=VMEM_LIMIT),
        name="moe_combine",
    )(tab, tab, x2, route, lp, g, y_rows)


def kernel(x, mem, positions, mix_norm_g, w_in, conv_dw_w, conv_dw_b, conv_ln_g, conv_ln_b,
           conv_out_g, attn_out_g, w_out, xattn_norm_g, mem_norm_g, w_xq, w_xk, w_xv, w_xo,
           moe_norm_g, w_group, b_group, w_router, b_router, w1, w3, w2, final_norm_g):
    bsz, seq, d = x.shape
    assert w_in.shape[0] == 1, "single-layer encoder only"
    l = 0
    t = bsz * seq
    vec = lambda a: a.reshape(1, -1)
    cos_t, sin_t = _rope_tables(positions)
    xf = x.reshape(t, d)
    c, *qkv = _in_projection(xf, vec(mix_norm_g[l]), w_in[l].astype(BF16), cos_t, sin_t, bsz, seq)
    mixc = _conformer_conv(c.reshape(bsz, seq, CONV_CH), conv_dw_w[l], vec(conv_dw_b[l]),
                           vec(conv_ln_g[l]), vec(conv_ln_b[l]), vec(conv_out_g[l]))
    branches = [_band_attention(*qkv[3 * i:3 * i + 3]) for i in range(len(DILATIONS))]
    kx, vx = _mem_kv(mem, vec(mem_norm_g[l]), w_xk[l].astype(BF16), w_xv[l].astype(BF16))
    pad = LANES - N_GROUPS - N_EXPERTS
    w_rt = jnp.pad(jnp.concatenate([w_group[l], w_router[l]], axis=1), ((0, 0), (0, pad)))
    b_rt = jnp.pad(jnp.concatenate([b_group[l], b_router[l]]), (0, pad)).reshape(1, LANES)
    x2, h3, route, tile_tot = _post_mix(
        [o for o, _ in branches], [s for _, s in branches], mixc.reshape(t, CONV_CH), xf,
        vec(attn_out_g[l]), w_out[l].astype(BF16), vec(xattn_norm_g[l]), w_xq[l].astype(BF16),
        kx, vx, w_xo[l].astype(BF16), vec(moe_norm_g[l]), w_rt, b_rt, seq)

    assert TM_ROW == TM_POST
    bm = BM_EXPERT
    n_tile = t // TM_ROW
    tt = tile_tot[:, 0, :N_EXPERTS].astype(jnp.int32)
    tt = (tt + ROW_ALIGN - 1) // ROW_ALIGN * ROW_ALIGN
    lstart = jnp.cumsum(tt, axis=1) - tt
    before = jnp.cumsum(tt, axis=0) - tt
    pcounts = (jnp.sum(tt, axis=0) + bm - 1) // bm * bm
    pends = jnp.cumsum(pcounts).astype(jnp.int32)
    pstarts = pends - pcounts
    n_blk = (2 * t + (ROW_ALIGN - 1) * n_tile * N_EXPERTS) // bm + N_EXPERTS
    n_used = pends[-1:] // bm
    blk = jnp.minimum(jnp.arange(n_blk, dtype=jnp.int32), n_used - 1)
    blk_e = jnp.sum((blk[:, None] * bm >= pends[None, :]).astype(jnp.int32), axis=1)
    tile_rows = jnp.sum(tt, axis=1, keepdims=True)
    tab = jnp.concatenate([lstart, pstarts[None, :] + before, tt, tile_rows,
                           jnp.zeros((n_tile, N_EXPERTS - 1), jnp.int32)], axis=1)[:, None, :]
    lstart_v = jnp.pad(lstart.astype(F32), ((0, 0), (0, LANES - N_EXPERTS)))[:, None, :]

    x_rows, lp = _dispatch(pstarts, pends, tab, lstart_v, route, h3, n_blk * bm)
    y_rows = _experts(blk_e, n_used, x_rows, w1[l], w3[l], w2[l])
    out = _combine(tab, x2, route, lp, vec(final_norm_g), y_rows)
    return out.reshape(bsz, seq, d)
```

```python
import functools

import jax
import jax.numpy as jnp
from jax import lax
from jax.experimental import pallas as pl
from jax.experimental.pallas import tpu as pltpu

HEAD_DIM = 64
CONV_CH = 256
ATTN_HEADS = 12
ATTN_WIDTH = ATTN_HEADS * HEAD_DIM
CONV_KERNEL = 31
CONV_PAD = 16
BAND = 64
DILATIONS = (1, 4, 16)
ROPE_THETA = 10000.0
MEM_HEADS = 4
N_GROUPS = 4
EXPERTS_PER_GROUP = 8
N_EXPERTS = N_GROUPS * EXPERTS_PER_GROUP
NORM_EPS = 1e-6
LN_EPS = 1e-5
LANES = 128
NEG = -1e30
LOG2_E = 1.4426950408889634
LN_2 = 0.6931471805599453

TM_IN = 512
TM_POST = 512
POST_HALF = 512
TM_ROW = 512
BM_EXPERT = 512
CONV_CHUNK = 128
ATTN_SUB = 128
ATTN_Q_BLOCK = 512
VMEM_LIMIT = 56 * 1024 * 1024

F32 = jnp.float32
BF16 = jnp.bfloat16


def _rms(x, g):
    return x * lax.rsqrt(jnp.mean(x * x, axis=-1, keepdims=True) + NORM_EPS) * g


def _dot(a, b):
    return jnp.dot(a, b, preferred_element_type=F32)


def _dot_nt(a, b):
    return lax.dot_general(a, b, (((1,), (1,)), ((), ())), preferred_element_type=F32)


def _rope_kernel(pos_ref, inv_ref, sign_ref, cos_ref, sin_ref):
    ang = pos_ref[...].astype(F32) * inv_ref[...]
    cos_ref[...] = jnp.cos(ang)
    sin_ref[...] = jnp.sin(ang) * sign_ref[...]


def _rope_tables(positions):
    s = positions.shape[0]
    half = HEAD_DIM // 2
    inv = 1.0 / (ROPE_THETA ** (jnp.arange(half, dtype=F32) * (2.0 / HEAD_DIM)))
    inv = jnp.tile(inv, LANES // half)[None, :]
    sign = jnp.tile(jnp.concatenate([-jnp.ones((half,), F32), jnp.ones((half,), F32)]),
                    LANES // HEAD_DIM)[None, :]
    return pl.pallas_call(
        _rope_kernel,
        out_shape=(jax.ShapeDtypeStruct((s, LANES), F32), jax.ShapeDtypeStruct((s, LANES), F32)),
        name="rope_tables",
    )(positions.reshape(s, 1), inv, sign)


def _inproj_kernel(x_ref, g_ref, w_ref, cos_ref, sin_ref, c_ref, *rest):
    out_refs, slab_ref = rest[:-1], rest[-1]
    tm = x_ref.shape[0]
    n_grp = ATTN_WIDTH // LANES
    h = _rms(x_ref[...], g_ref[...]).astype(BF16)
    u = _dot(h, w_ref[:, 0:2 * CONV_CH])
    c_ref[...] = u[:, :CONV_CH] / (1.0 + jnp.exp(-u[:, CONV_CH:]))
    cos = cos_ref[...]
    sin = sin_ref[...]
    lane = lax.broadcasted_iota(jnp.int32, cos.shape, 1)
    first_half = (lane % HEAD_DIM) < (HEAD_DIM // 2)
    off = 2 * CONV_CH
    for which, scale in enumerate((HEAD_DIM ** -0.5 * LOG2_E, 1.0, None)):
        u = _dot(h, w_ref[:, off:off + ATTN_WIDTH])
        off += ATTN_WIDTH
        for j in range(n_grp):
            xs = u[:, j * LANES:(j + 1) * LANES]
            if scale is not None:
                partner = jnp.where(first_half, pltpu.roll(xs, LANES - 32, 1),
                                    pltpu.roll(xs, 32, 1))
                xs = (xs * cos + partner * sin) * scale
            slab_ref[which * n_grp + j] = xs
        for di, dil in enumerate(DILATIONS):
            ref = out_refs[3 * di + which]
            n = tm // dil
            for r in range(dil):
                for j in range(n_grp):
                    idx = which * n_grp + j
                    rows = slab_ref[idx] if dil == 1 else slab_ref[idx, pl.ds(r, n, stride=dil), :]
                    ref[0, r, :, j * LANES:(j + 1) * LANES] = rows.astype(BF16)


def _in_projection(xf, g, w_in, cos_t, sin_t, bsz, seq):
    t, d = xf.shape
    tm = TM_IN
    n_pos_blk = seq // tm
    row = lambda i: (i, 0)
    const = lambda i: (0, 0)
    cls = lambda i: (i // n_pos_blk, 0, i % n_pos_blk, 0)
    out_specs = [pl.BlockSpec((tm, CONV_CH), row)]
    out_shape = [jax.ShapeDtypeStruct((t, CONV_CH), F32)]
    for dil in DILATIONS:
        for _ in range(3):
            out_specs.append(pl.BlockSpec((1, dil, tm // dil, ATTN_WIDTH), cls))
            out_shape.append(jax.ShapeDtypeStruct((bsz, dil, seq // dil, ATTN_WIDTH), BF16))
    return pl.pallas_call(
        _inproj_kernel,
        grid=(t // tm,),
        in_specs=[
            pl.BlockSpec((tm, d), row),
            pl.BlockSpec((1, d), const),
            pl.BlockSpec(w_in.shape, const),
            pl.BlockSpec((tm, LANES), lambda i: (i % n_pos_blk, 0)),
            pl.BlockSpec((tm, LANES), lambda i: (i % n_pos_blk, 0)),
        ],
        out_specs=out_specs,
        out_shape=out_shape,
        scratch_shapes=[pltpu.VMEM((3 * ATTN_WIDTH // LANES, tm, LANES), F32)],
        compiler_params=pltpu.CompilerParams(
            dimension_semantics=("parallel",), vmem_limit_bytes=VMEM_LIMIT),
        name="in_projection",
    )(xf, g, w_in, cos_t, sin_t)


def _conv_kernel(c_ref, w_ref, b_ref, lng_ref, lnb_ref, og_ref, o_ref, pad_ref):
    seq = c_ref.shape[1]
    n_slab = CONV_CH // LANES
    zeros = jnp.zeros((CONV_PAD, LANES), F32)
    for h in range(n_slab):
        pad_ref[h, 0:CONV_PAD, :] = zeros
        pad_ref[h, seq + CONV_PAD:seq + 2 * CONV_PAD, :] = zeros
        pad_ref[h, CONV_PAD:seq + CONV_PAD, :] = c_ref[0, :, h * LANES:(h + 1) * LANES]
    shift = CONV_PAD - CONV_KERNEL // 2

    def body(i, carry):
        base = pl.multiple_of(i * CONV_CHUNK, CONV_CHUNK)
        acc = []
        for h in range(n_slab):
            sl = slice(h * LANES, (h + 1) * LANES)
            a = jnp.zeros((CONV_CHUNK, LANES), F32)
            for k in range(CONV_KERNEL):
                a = a + pad_ref[h, pl.ds(base + (k + shift), CONV_CHUNK), :] * w_ref[k:k + 1, sl]
            acc.append(a + b_ref[:, sl])
        inv_ch = 1.0 / CONV_CH
        mu = sum(jnp.sum(a, axis=-1, keepdims=True) for a in acc) * inv_ch
        cen = [a - mu for a in acc]
        var = sum(jnp.sum(c * c, axis=-1, keepdims=True) for c in cen) * inv_ch
        rstd = lax.rsqrt(var + LN_EPS)
        ys = []
        for h, c in enumerate(cen):
            sl = slice(h * LANES, (h + 1) * LANES)
            y = c * rstd * lng_ref[:, sl] + lnb_ref[:, sl]
            ys.append(y / (1.0 + jnp.exp(-y)))
        ms = sum(jnp.sum(y * y, axis=-1, keepdims=True) for y in ys) * inv_ch
        rinv = lax.rsqrt(ms + NORM_EPS)
        for h, y in enumerate(ys):
            sl = slice(h * LANES, (h + 1) * LANES)
            o_ref[0, pl.ds(base, CONV_CHUNK), sl] = (y * rinv * og_ref[:, sl]).astype(BF16)
        return carry

    lax.fori_loop(0, seq // CONV_CHUNK, body, 0, unroll=2)


def _conformer_conv(c, w, b, lng, lnb, og):
    bsz, seq, ch = c.shape
    vec = pl.BlockSpec((1, ch), lambda i: (0, 0))
    return pl.pallas_call(
        _conv_kernel,
        grid=(bsz,),
        in_specs=[pl.BlockSpec((1, seq, ch), lambda i: (i, 0, 0)),
                  pl.BlockSpec(w.shape, lambda i: (0, 0)), vec, vec, vec, vec],
        out_specs=pl.BlockSpec((1, seq, ch), lambda i: (i, 0, 0)),
        out_shape=jax.ShapeDtypeStruct((bsz, seq, ch), BF16),
        scratch_shapes=[pltpu.VMEM((ch // LANES, seq + 2 * CONV_PAD, LANES), F32)],
        compiler_params=pltpu.CompilerParams(
            dimension_semantics=("parallel",), vmem_limit_bytes=VMEM_LIMIT),
        name="conformer_conv",
    )(c, w, b, lng, lnb, og)


def _band_attn_kernel(q_ref, k_ref, v_ref, o_ref, lse_ref, *, sub_len, q_blk, win):
    sq = ATTN_SUB
    row = lax.broadcasted_iota(jnp.int32, (2 * sq, win), 0) % sq
    col = lax.broadcasted_iota(jnp.int32, (2 * sq, win), 1)
    lane = lax.broadcasted_iota(jnp.int32, (sq, LANES), 1)
    is_a = lane < HEAD_DIM
    def chain(rows, ws, bias, hp):
        sl = slice(hp * LANES, (hp + 1) * LANES)
        qh = q_ref[rows, sl]
        kh = k_ref[pl.ds(ws, win), sl]
        zero = jnp.zeros_like(qh)
        q2 = jnp.concatenate([jnp.where(is_a, qh, zero), jnp.where(is_a, zero, qh)], axis=0)
        s = _dot_nt(q2, kh) + bias
        yield
        m = jnp.max(s, axis=1, keepdims=True)
        p = jnp.exp2(s - m)
        l = jnp.sum(p, axis=1, keepdims=True)
        yield
        o2 = _dot(p.astype(BF16), v_ref[pl.ds(ws, win), sl]) / l
        lse2 = m * LN_2 + jnp.log(l)
        o_ref[rows, sl] = jnp.where(is_a, o2[:sq], o2[sq:]).astype(BF16)
        lse_ref[rows, sl] = jnp.where(is_a, lse2[:sq], lse2[sq:])
        yield

    chains = []
    for sub in range(q_blk // sq):
        q0 = pl.program_id(2) * q_blk + sub * sq
        ws = pl.multiple_of(jnp.clip(q0 - BAND, 0, sub_len - win), BAND)
        bias = jnp.where(jnp.abs(row - col + (q0 - ws)) <= BAND, 0.0, NEG).astype(F32)
        rows = slice(sub * sq, (sub + 1) * sq)
        chains += [chain(rows, ws, bias, hp) for hp in range(ATTN_WIDTH // LANES)]
    n_stage = 3
    for step in range(n_stage + len(chains) - 1):
        for ci in reversed(range(len(chains))):
            if 0 <= step - ci < n_stage:
                next(chains[ci])


def _band_attention(q, k, v):
    bsz, dil, sub_len, width = q.shape
    q_blk = min(ATTN_Q_BLOCK, sub_len)
    win = min(sub_len, ATTN_SUB + 2 * BAND)
    qmap = lambda b, r, i: (b, r, i, 0)
    kmap = lambda b, r, i: (b, r, 0, 0)
    return pl.pallas_call(
        functools.partial(_band_attn_kernel, sub_len=sub_len, q_blk=q_blk, win=win),
        grid=(bsz, dil, sub_len // q_blk),
        in_specs=[pl.BlockSpec((None, None, q_blk, width), qmap),
                  pl.BlockSpec((None, None, sub_len, width), kmap),
                  pl.BlockSpec((None, None, sub_len, width), kmap)],
        out_specs=[pl.BlockSpec((None, None, q_blk, width), qmap),
                   pl.BlockSpec((None, None, q_blk, width), qmap)],
        out_shape=[jax.ShapeDtypeStruct((bsz, dil, sub_len, width), BF16),
                   jax.ShapeDtypeStruct((bsz, dil, sub_len, width), F32)],
        compiler_params=pltpu.CompilerParams(
            dimension_semantics=("parallel", "parallel", "arbitrary"),
            vmem_limit_bytes=VMEM_LIMIT),
        name=f"band_attention_d{dil}",
    )(q, k, v)


def _memkv_kernel(m_ref, g_ref, wk_ref, wv_ref, k_ref, v_ref):
    h = _rms(m_ref[0], g_ref[...]).astype(BF16)
    k_ref[0] = _dot(h, wk_ref[...]).astype(BF16)
    v_ref[0] = _dot(h, wv_ref[...]).astype(BF16)


def _mem_kv(mem, g, wk, wv):
    bsz, m, d = mem.shape
    blk = pl.BlockSpec((1, m, d), lambda i: (i, 0, 0))
    const = lambda i: (0, 0)
    return pl.pallas_call(
        _memkv_kernel,
        grid=(bsz,),
        in_specs=[blk, pl.BlockSpec((1, d), const), pl.BlockSpec((d, d), const),
                  pl.BlockSpec((d, d), const)],
        out_specs=[blk, blk],
        out_shape=[jax.ShapeDtypeStruct((bsz, m, d), BF16)] * 2,
        compiler_params=pltpu.CompilerParams(dimension_semantics=("parallel",)),
        name="mem_kv",
    )(mem, g, wk, wv)


def _post_kernel(o1_ref, o2_ref, o3_ref, l1_ref, l2_ref, l3_ref, mc_ref, x_ref,
                 ag_ref, wout_ref, xg_ref, wq_ref, kx_ref, vx_ref, wo_ref,
                 mg_ref, wr_ref, br_ref,
                 x2_ref, h3_ref, route_ref, tot_ref,
                 slab_ref, *scratch):
    tm = x_ref.shape[0]
    d = x_ref.shape[1]
    n_grp = ATTN_WIDTH // LANES

    hm = POST_HALF
    lane = lax.broadcasted_iota(jnp.int32, (hm, LANES), 1)
    is_a = lane < HEAD_DIM
    w_rt = wr_ref[...]
    w_hi = w_rt.astype(BF16)
    w_lo = (w_rt - w_hi.astype(F32)).astype(BF16)

    def first_max(vals):
        vmax = jnp.max(vals, axis=1, keepdims=True)
        idx = jnp.min(jnp.where(vals == vmax, lane, LANES), axis=1, keepdims=True)
        return vmax, idx

    tot_box = [jnp.zeros((1, LANES), F32)]

    def chain(r0, mixed_ref, ox_ref):
        rs = slice(r0, r0 + hm)

        for bi, (o_ref, l_ref) in enumerate(((o2_ref, l2_ref), (o3_ref, l3_ref))):
            dil = o_ref.shape[1]
            n = hm // dil
            src = slice(r0 // dil, r0 // dil + n)
            base = bi * 2 * n_grp
            for r in range(dil):
                rows = pl.ds(r0 + r, n, stride=dil)
                for j in range(n_grp):
                    sl = slice(j * LANES, (j + 1) * LANES)
                    slab_ref[base + j, rows, :] = o_ref[0, r, src, sl].astype(F32)
                    slab_ref[base + n_grp + j, rows, :] = l_ref[0, r, src, sl]
        yield

        pieces = []
        ssq = jnp.zeros((hm, 1), F32)
        for hp in range(n_grp):
            sl = slice(hp * LANES, (hp + 1) * LANES)
            e1 = l1_ref[0, 0, rs, sl]
            e2 = slab_ref[n_grp + hp, rs, :]
            e3 = slab_ref[3 * n_grp + hp, rs, :]
            mx = jnp.maximum(jnp.maximum(e1, e2), e3)
            w1, w2, w3 = jnp.exp(e1 - mx), jnp.exp(e2 - mx), jnp.exp(e3 - mx)
            num = (w1 * o1_ref[0, 0, rs, sl].astype(F32) + w2 * slab_ref[hp, rs, :]
                   + w3 * slab_ref[2 * n_grp + hp, rs, :])
            a = num / (w1 + w2 + w3)
            pieces.append(a)
            ssq = ssq + jnp.sum(a * a, axis=1, keepdims=True)
        inv = lax.rsqrt(ssq * (1.0 / ATTN_WIDTH) + NORM_EPS)
        mixed_ref[:, 0:CONV_CH] = mc_ref[rs, :]
        for hp, a in enumerate(pieces):
            sl = slice(hp * LANES, (hp + 1) * LANES)
            mixed_ref[:, CONV_CH + hp * LANES:CONV_CH + (hp + 1) * LANES] = (
                a * inv * ag_ref[:, sl]).astype(BF16)
        yield
        x1 = x_ref[rs, :] + _dot(mixed_ref[...], wout_ref[...])
        yield

        hd = d // MEM_HEADS
        qx = (_dot(_rms(x1, xg_ref[...]).astype(BF16), wq_ref[...]) * (hd ** -0.5)).astype(BF16)
        yield
        for h in range(MEM_HEADS):
            sl = slice(h * hd, (h + 1) * hd)
            s = _dot_nt(qx[:, sl], kx_ref[0, :, sl])
            m = jnp.max(s, axis=1, keepdims=True)
            p = jnp.exp(s - m)
            l = jnp.sum(p, axis=1, keepdims=True)
            ox_ref[:, sl] = (_dot(p.astype(BF16), vx_ref[0, :, sl]) / l).astype(BF16)
        yield
        x2 = x1 + _dot(ox_ref[...], wo_ref[...])
        x2_ref[rs, :] = x2
        yield

        h3 = _rms(x2, mg_ref[...])
        h3_ref[rs, :] = h3.astype(BF16)
        h_hi = h3.astype(BF16)
        h_lo = (h3 - h_hi.astype(F32)).astype(BF16)
        lg = _dot(h_hi, w_hi) + (_dot(h_hi, w_lo) + _dot(h_lo, w_hi)) + br_ref[...]

        is_group = lane < N_GROUPS
        gmax, gidx = first_max(jnp.where(is_group, lg, NEG))
        p_g = 1.0 / jnp.sum(jnp.where(is_group, jnp.exp(lg - gmax), 0.0), axis=1, keepdims=True)
        lo = N_GROUPS + EXPERTS_PER_GROUP * gidx
        el = jnp.where((lane >= lo) & (lane < lo + EXPERTS_PER_GROUP), lg, NEG)
        v1, i1 = first_max(el)
        v2, i2 = first_max(jnp.where(lane == i1, NEG, el))
        t2 = jnp.exp(v2 - v1)
        g1 = p_g / (1.0 + t2)
        g2 = g1 * t2
        e1 = i1 - N_GROUPS
        e2 = i2 - N_GROUPS

        hits = jnp.where((lane == e1) | (lane == e2), 1.0, 0.0)
        tot_box[0] = tot_box[0] + jnp.sum(hits, axis=0, keepdims=True)

        route = jnp.zeros((hm, LANES), F32)
        for pos, val in enumerate((e1.astype(F32), e2.astype(F32), g1, g2)):
            route = jnp.where(lane == pos, val, route)
        route_ref[rs, :] = route
        yield

    chains = [chain(i * hm, scratch[2 * i], scratch[2 * i + 1]) for i in range(tm // hm)]
    n_stage = 7
    for step in range(n_stage + len(chains) - 1):
        for ci, ch in enumerate(chains):
            if 0 <= step - ci < n_stage:
                next(ch)
    tot_ref[0] = tot_box[0]


def _post_mix(o_list, lse_list, mixc, xf, ag, w_out, xg, w_xq, kx, vx, w_xo, mg, w_rt, b_rt, seq):
    t, d = xf.shape
    tm = TM_POST
    blk_per_seq = seq // tm
    row = lambda i: (i, 0)
    const = lambda i: (0, 0)
    bmap = lambda i: (i // blk_per_seq, 0, 0)
    mlen = kx.shape[1]
    cls = lambda i: (i // blk_per_seq, 0, i % blk_per_seq, 0)
    wide = [pl.BlockSpec((1, dil, tm // dil, ATTN_WIDTH), cls) for dil in DILATIONS]
    narrow = pl.BlockSpec((tm, LANES), row)
    full = pl.BlockSpec((tm, d), row)
    mat = pl.BlockSpec((d, d), const)
    n_slab = 4 * (ATTN_WIDTH // LANES)
    return pl.pallas_call(
        _post_kernel,
        grid=(t // tm,),
        in_specs=[*wide, *wide,
                  pl.BlockSpec((tm, CONV_CH), row), full,
                  pl.BlockSpec((1, ATTN_WIDTH), const), mat,
                  pl.BlockSpec((1, d), const), mat,
                  pl.BlockSpec((1, mlen, d), bmap), pl.BlockSpec((1, mlen, d), bmap), mat,
                  pl.BlockSpec((1, d), const), pl.BlockSpec((d, LANES), const),
                  pl.BlockSpec((1, LANES), const)],
        out_specs=[full, full, narrow, pl.BlockSpec((1, 1, LANES), lambda i: (i, 0, 0))],
        out_shape=[jax.ShapeDtypeStruct((t, d), F32), jax.ShapeDtypeStruct((t, d), BF16),
                   jax.ShapeDtypeStruct((t, LANES), F32),
                   jax.ShapeDtypeStruct((t // tm, 1, LANES), F32)],
        scratch_shapes=[pltpu.VMEM((n_slab, tm, LANES), F32)]
        + [pltpu.VMEM((POST_HALF, d), BF16)] * (2 * (tm // POST_HALF)),
        compiler_params=pltpu.CompilerParams(
            dimension_semantics=("parallel",), vmem_limit_bytes=VMEM_LIMIT),
        name="post_mix",
    )(*o_list, *lse_list, mixc, xf, ag, w_out, xg, w_xq, kx, vx, w_xo, mg, w_rt, b_rt)


ROW_ALIGN = 8
RUN_PIECES = tuple(2 ** k for k in range(9, 2, -1))
TILE_PIECES = (1024,) + RUN_PIECES
LOCAL_ROWS = 2 * TM_ROW + ROW_ALIGN * N_EXPERTS


def _run_copies(tab_ref, local_ref, global_ref, sem, to_global):
    def body(e, carry):
        ls = tab_ref[0, 0, e]
        gs = tab_ref[0, 0, N_EXPERTS + e]
        n = tab_ref[0, 0, 2 * N_EXPERTS + e]
        for pi, size in enumerate(RUN_PIECES):
            done = (n // (2 * size)) * (2 * size)

            @pl.when((n & size) != 0)
            def _():
                loc = local_ref.at[pl.ds(pl.multiple_of(ls + done, ROW_ALIGN), size)]
                glo = global_ref.at[pl.ds(pl.multiple_of(gs + done, ROW_ALIGN), size)]
                src, dst = (loc, glo) if to_global else (glo, loc)
                pltpu.make_async_copy(src, dst, sem).start(priority=pi % 2)
        return carry

    lax.fori_loop(0, N_EXPERTS, body, 0)


def _run_waits(total, local_ref, global_ref, sem):
    for size in TILE_PIECES:
        @pl.when((total & size) != 0)
        def _():
            pltpu.make_async_copy(local_ref.at[pl.ds(0, size)], global_ref.at[pl.ds(0, size)],
                                  sem).wait()


def _local_positions(route, lstart):
    tm = route.shape[0]
    lane = lax.broadcasted_iota(jnp.int32, (tm, LANES), 1)
    rr = lax.broadcasted_iota(jnp.int32, (tm, tm), 0)
    cc = lax.broadcasted_iota(jnp.int32, (tm, tm), 1)
    ltri = jnp.where(cc < rr, 1.0, 0.0).astype(BF16)
    oh1 = jnp.where(lane == route[:, 0:1].astype(jnp.int32), 1.0, 0.0)
    oh2 = jnp.where(lane == route[:, 1:2].astype(jnp.int32), 1.0, 0.0)
    pre1 = _dot(ltri, oh1.astype(BF16))
    pre2 = _dot(ltri, oh2.astype(BF16))
    tot1 = jnp.sum(oh1, axis=0, keepdims=True)
    lp1 = jnp.sum(oh1 * (lstart + pre1), axis=1, keepdims=True)
    lp2 = jnp.sum(oh2 * (lstart + tot1 + pre2), axis=1, keepdims=True)
    return lp1, lp2


def _dispatch_kernel(ps_ref, pe_ref, tab_ref, ls_ref, route_ref, h_ref, xout_ref, lp_ref,
                     zero_ref, xs_ref, pend_ref, sems):
    tm = h_ref.shape[0]
    bm = zero_ref.shape[0]
    sem = sems.at[2]

    @pl.when(pl.program_id(0) == 0)
    def _():
        zero_ref[...] = jnp.zeros_like(zero_ref)

        def pad_copy(e):
            last = pl.multiple_of(pe_ref[e] - bm, bm)
            return pltpu.make_async_copy(zero_ref, xout_ref.at[pl.ds(last, bm)], sem)

        def nonempty(e):
            return pe_ref[e] > ps_ref[e]

        def start(e, carry):
            pl.when(nonempty(e))(lambda: pad_copy(e).start())
            return carry

        def wait(e, carry):
            pl.when(nonempty(e))(lambda: pad_copy(e).wait())
            return carry

        lax.fori_loop(0, N_EXPERTS, start, 0)
        lax.fori_loop(0, N_EXPERTS, wait, 0)

        def tail_copy(b):
            return pltpu.make_async_copy(
                zero_ref, xout_ref.at[pl.ds(pl.multiple_of(b * bm, bm), bm)], sem)

        def tail_start(b, carry):
            tail_copy(b).start()
            return carry

        def tail_wait(b, carry):
            tail_copy(b).wait()
            return carry

        first_tail = pe_ref[N_EXPERTS - 1] // bm
        n_blk = xout_ref.shape[0] // bm
        lax.fori_loop(first_tail, n_blk, tail_start, 0)
        lax.fori_loop(first_tail, n_blk, tail_wait, 0)

    i = pl.program_id(0)
    n_step = pl.num_programs(0)
    slot = i % 2

    def drain(s):
        _run_waits(pend_ref[s], xs_ref.at[s], xout_ref, sems.at[s])

    @pl.when(i >= 2)
    def _():
        drain(slot)

    lp1, lp2 = _local_positions(route_ref[...], ls_ref[0])
    lane = lax.broadcasted_iota(jnp.int32, (tm, LANES), 1)
    lp_ref[...] = jnp.where(lane == 0, lp1, jnp.where(lane == 1, lp2, 0.0))
    row1 = jnp.broadcast_to(lp1, (tm, LANES)).T[0:1, :].astype(jnp.int32)
    row2 = jnp.broadcast_to(lp2, (tm, LANES)).T[0:1, :].astype(jnp.int32)
    slot_id = lax.broadcasted_iota(jnp.int32, (LOCAL_ROWS, tm), 0)
    perm = jnp.where((slot_id == row1) | (slot_id == row2), 1.0, 0.0).astype(BF16)
    xs_ref[slot] = _dot(perm, h_ref[...])
    _run_copies(tab_ref, xs_ref.at[slot], xout_ref, sems.at[slot], to_global=True)
    pend_ref[slot] = tab_ref[0, 0, 3 * N_EXPERTS]

    @pl.when(i == n_step - 1)
    def _():
        drain(slot)

        @pl.when(n_step > 1)
        def _():
            drain(1 - slot)


def _dispatch(pstarts, pends, tab, lstart, route, h3, n_rows):
    t, d = h3.shape
    tm = TM_ROW
    tile = lambda i, ps, pe: (i, 0, 0)
    row = lambda i, ps, pe: (i, 0)
    return pl.pallas_call(
        _dispatch_kernel,
        grid_spec=pltpu.PrefetchScalarGridSpec(
            num_scalar_prefetch=2,
            grid=(t // tm,),
            in_specs=[pl.BlockSpec((1, 1, LANES), tile, memory_space=pltpu.SMEM),
                      pl.BlockSpec((1, 1, LANES), tile),
                      pl.BlockSpec((tm, LANES), row),
                      pl.BlockSpec((tm, d), row)],
            out_specs=[pl.BlockSpec(memory_space=pl.ANY), pl.BlockSpec((tm, LANES), row)],
            scratch_shapes=[pltpu.VMEM((BM_EXPERT, d), F32), pltpu.VMEM((2, LOCAL_ROWS, d), F32),
                            pltpu.SMEM((2,), jnp.int32), pltpu.SemaphoreType.DMA((3,))]),
        out_shape=[jax.ShapeDtypeStruct((n_rows, d), F32),
                   jax.ShapeDtypeStruct((t, LANES), F32)],
        compiler_params=pltpu.CompilerParams(
            dimension_semantics=("arbitrary",), vmem_limit_bytes=VMEM_LIMIT),
        name="moe_dispatch",
    )(pstarts, pends, tab, lstart, route, h3)


def _expert_kernel(be_ref, nu_ref, x_ref, w1_ref, w3_ref, w2_ref, y_ref, w1b, w3b, w2b):
    i = pl.program_id(0)
    prev = be_ref[jnp.maximum(i - 1, 0)]

    @pl.when((i == 0) | (be_ref[i] != prev))
    def _():
        w1b[...] = w1_ref[0].astype(BF16)
        w3b[...] = w3_ref[0].astype(BF16)
        w2b[...] = w2_ref[0].astype(BF16)

    @pl.when(i < nu_ref[0])
    def _():
        xb = x_ref[...].astype(BF16)
        a = _dot(xb, w1b[...])
        g = _dot(xb, w3b[...])
        act = (a / (1.0 + jnp.exp(-a)) * g).astype(BF16)
        y_ref[...] = _dot(act, w2b[...])

    @pl.when(i >= nu_ref[0])
    def _():
        y_ref[...] = jnp.zeros_like(y_ref)


def _experts(blk_e, n_used, x_rows, w1, w3, w2):
    p, d = x_rows.shape
    ff = w1.shape[2]
    bm = BM_EXPERT
    wmap = lambda i, be, nu: (be[i], 0, 0)
    return pl.pallas_call(
        _expert_kernel,
        grid_spec=pltpu.PrefetchScalarGridSpec(
            num_scalar_prefetch=2,
            grid=(p // bm,),
            in_specs=[pl.BlockSpec((bm, d), lambda i, be, nu: (jnp.minimum(i, nu[0] - 1), 0)),
                      pl.BlockSpec((1, d, ff), wmap),
                      pl.BlockSpec((1, d, ff), wmap),
                      pl.BlockSpec((1, ff, d), wmap)],
            out_specs=pl.BlockSpec((bm, d), lambda i, be, nu: (i, 0)),
            scratch_shapes=[pltpu.VMEM((d, ff), BF16), pltpu.VMEM((d, ff), BF16),
                            pltpu.VMEM((ff, d), BF16)]),
        out_shape=jax.ShapeDtypeStruct((p, d), F32),
        compiler_params=pltpu.CompilerParams(
            dimension_semantics=("arbitrary",), vmem_limit_bytes=VMEM_LIMIT),
        name="moe_experts",
    )(blk_e, n_used, x_rows, w1, w3, w2)


def _combine_kernel(tab_ref, nxt_ref, x2_ref, route_ref, lp_ref, g_ref, y_ref, o_ref, ys_ref, sems):
    tm = x2_ref.shape[0]
    i = pl.program_id(0)
    n_step = pl.num_programs(0)
    slot = i % 2

    @pl.when(i == 0)
    def _():
        ys_ref[...] = jnp.zeros_like(ys_ref)
        _run_copies(tab_ref, ys_ref.at[0], y_ref, sems.at[0], to_global=False)

    @pl.when(i + 1 < n_step)
    def _():
        _run_copies(nxt_ref, ys_ref.at[1 - slot], y_ref, sems.at[1 - slot], to_global=False)

    _run_waits(tab_ref[0, 0, 3 * N_EXPERTS], ys_ref.at[slot], y_ref, sems.at[slot])

    route = route_ref[...]
    lp = lp_ref[...]
    slot_id = lax.broadcasted_iota(jnp.int32, (tm, LOCAL_ROWS), 1)
    sel = (jnp.where(slot_id == lp[:, 0:1].astype(jnp.int32), route[:, 2:3], 0.0)
           + jnp.where(slot_id == lp[:, 1:2].astype(jnp.int32), route[:, 3:4], 0.0))
    x3 = x2_ref[...] + _dot(sel.astype(BF16), ys_ref[slot].astype(BF16))
    o_ref[...] = _rms(x3, g_ref[...])


def _combine(tab, x2, route, lp, g, y_rows):
    t, d = x2.shape
    tm = TM_ROW
    n_tile = t // tm
    row = lambda i: (i, 0)
    return pl.pallas_call(
        _combine_kernel,
        grid=(n_tile,),
        in_specs=[pl.BlockSpec((1, 1, LANES), lambda i: (i, 0, 0), memory_space=pltpu.SMEM),
                  pl.BlockSpec((1, 1, LANES), lambda i: (jnp.minimum(i + 1, n_tile - 1), 0, 0),
                               memory_space=pltpu.SMEM),
                  pl.BlockSpec((tm, d), row),
                  pl.BlockSpec((tm, LANES), row),
                  pl.BlockSpec((tm, LANES), row),
                  pl.BlockSpec((1, d), lambda i: (0, 0)),
                  pl.BlockSpec(memory_space=pl.ANY)],
        out_specs=pl.BlockSpec((tm, d), row),
        out_shape=jax.ShapeDtypeStruct((t, d), F32),
        scratch_shapes=[pltpu.VMEM((2, LOCAL_ROWS, d), F32), pltpu.SemaphoreType.DMA((2,))],
        compiler_params=pltpu.CompilerParams(
            dimension_semantics=("arbitrary",), vmem_limit_bytes=VMEM_LIMIT),
        name="moe_combine",
    )(tab, tab, x2, route, lp, g, y_rows)


def kernel(x, mem, positions, mix_norm_g, w_in, conv_dw_w, conv_dw_b, conv_ln_g, conv_ln_b,
           conv_out_g, attn_out_g, w_out, xattn_norm_g, mem_norm_g, w_xq, w_xk, w_xv, w_xo,
           moe_norm_g, w_group, b_group, w_router, b_router, w1, w3, w2, final_norm_g):
    bsz, seq, d = x.shape
    assert w_in.shape[0] == 1, "single-layer encoder only"
    l = 0
    t = bsz * seq
    vec = lambda a: a.reshape(1, -1)
    cos_t, sin_t = _rope_tables(positions)
    xf = x.reshape(t, d)
    c, *qkv = _in_projection(xf, vec(mix_norm_g[l]), w_in[l].astype(BF16), cos_t, sin_t, bsz, seq)
    mixc = _conformer_conv(c.reshape(bsz, seq, CONV_CH), conv_dw_w[l], vec(conv_dw_b[l]),
                           vec(conv_ln_g[l]), vec(conv_ln_b[l]), vec(conv_out_g[l]))
    branches = [_band_attention(*qkv[3 * i:3 * i + 3]) for i in range(len(DILATIONS))]
    kx, vx = _mem_kv(mem, vec(mem_norm_g[l]), w_xk[l].astype(BF16), w_xv[l].astype(BF16))
    pad = LANES - N_GROUPS - N_EXPERTS
    w_rt = jnp.pad(jnp.concatenate([w_group[l], w_router[l]], axis=1), ((0, 0), (0, pad)))
    b_rt = jnp.pad(jnp.concatenate([b_group[l], b_router[l]]), (0, pad)).reshape(1, LANES)
    x2, h3, route, tile_tot = _post_mix(
        [o for o, _ in branches], [s for _, s in branches], mixc.reshape(t, CONV_CH), xf,
        vec(attn_out_g[l]), w_out[l].astype(BF16), vec(xattn_norm_g[l]), w_xq[l].astype(BF16),
        kx, vx, w_xo[l].astype(BF16), vec(moe_norm_g[l]), w_rt, b_rt, seq)

    assert TM_ROW == TM_POST
    bm = BM_EXPERT
    n_tile = t // TM_ROW
    tt = tile_tot[:, 0, :N_EXPERTS].astype(jnp.int32)
    tt = (tt + ROW_ALIGN - 1) // ROW_ALIGN * ROW_ALIGN
    lstart = jnp.cumsum(tt, axis=1) - tt
    before = jnp.cumsum(tt, axis=0) - tt
    pcounts = (jnp.sum(tt, axis=0) + bm - 1) // bm * bm
    pends = jnp.cumsum(pcounts).astype(jnp.int32)
    pstarts = pends - pcounts
    n_blk = (2 * t + (ROW_ALIGN - 1) * n_tile * N_EXPERTS) // bm + N_EXPERTS
    n_used = pends[-1:] // bm
    blk = jnp.minimum(jnp.arange(n_blk, dtype=jnp.int32), n_used - 1)
    blk_e = jnp.sum((blk[:, None] * bm >= pends[None, :]).astype(jnp.int32), axis=1)
    tile_rows = jnp.sum(tt, axis=1, keepdims=True)
    tab = jnp.concatenate([lstart, pstarts[None, :] + before, tt, tile_rows,
                           jnp.zeros((n_tile, N_EXPERTS - 1), jnp.int32)], axis=1)[:, None, :]
    lstart_v = jnp.pad(lstart.astype(F32), ((0, 0), (0, LANES - N_EXPERTS)))[:, None, :]

    x_rows, lp = _dispatch(pstarts, pends, tab, lstart_v, route, h3, n_blk * bm)
    y_rows = _experts(blk_e, n_used, x_rows, w1[l], w3[l], w2[l])
    out = _combine(tab, x2, route, lp, vec(final_norm_g), y_rows)
    return out.reshape(bsz, seq, d)
```
